```python
import math
import jax
import jax.numpy as jnp
from jax import lax
import numpy as np

D_MODEL = 1024
BATCH = 4
SEQ = 4096
DEPTH = 2
DEC_BATCH = 128
DEC_SEQ = 1
PAST_LEN = 2048
PAGE_SIZE = 128

H_A = 4
DH_A = 64
H_B = 4
DK_B = 128
DV_B = 128
H_C = 8
DH_C = 64
H_D = 4
DK_D = 64
DV_D = 128
D_FF = 4 * D_MODEL
NUM_BUCKETS = 32
MAX_DISTANCE = 128
Q_BLOCK = 128
CHUNK = 64
EPS = 1e-6
NEG = -1e30
N_EVEN = (DEPTH + 1) // 2
N_ODD = DEPTH // 2
W_A = H_A * 2 * DH_A
W_B = H_B * DV_B
W_C = H_C * DH_C
W_D = H_D * DV_D
SPLIT_EVEN = (W_A, W_A, W_A, H_B * DK_B, H_B * DK_B, W_B, W_B)
SPLIT_ODD = (W_C, W_C, W_C, H_C, H_D * DK_D, H_D * DK_D, W_D, H_D, H_D, W_D)
D_IN_EVEN = sum(SPLIT_EVEN)
D_IN_ODD = sum(SPLIT_ODD)

kernel_name = 'hybrid_diffattn_hgrn2_fox_mlstm_step'


def _rmsnorm(x, g):
    xf = x.astype(jnp.float32)
    y = xf * lax.rsqrt(jnp.mean(xf * xf, axis=-1, keepdims=True) + EPS)
    return (y * g.astype(jnp.float32)).astype(x.dtype)


def _split(x, sizes):
    return jnp.split(x, np.cumsum(sizes)[:-1].tolist(), axis=-1)


def _gather_pages(cache, page_table):
    g = cache[page_table]
    return g.reshape(g.shape[0], g.shape[1] * g.shape[2], *g.shape[3:])


def _t5_bias(q_pos, k_pos, table):
    n = jnp.maximum(q_pos[:, None] - k_pos[None, :], 0)
    max_exact = NUM_BUCKETS // 2
    nf = jnp.maximum(n, 1).astype(jnp.float32)
    large = max_exact + (jnp.log(nf / max_exact) / math.log(MAX_DISTANCE / max_exact)
                         * (NUM_BUCKETS - max_exact)).astype(jnp.int32)
    large = jnp.minimum(large, NUM_BUCKETS - 1)
    bucket = jnp.where(n < max_exact, n, large)
    return jnp.transpose(table[bucket].astype(jnp.float32), (2, 0, 1))


def _sweep_blocks(fn, qs, q_pos):
    T = q_pos.shape[0]
    blk = Q_BLOCK if T % Q_BLOCK == 0 else T
    nb = T // blk
    qb = tuple(jnp.moveaxis(q.reshape(q.shape[0], nb, blk, *q.shape[2:]), 1, 0) for q in qs)
    pb = q_pos.reshape(nb, blk)
    out = lax.map(lambda a: fn(*a[0], a[1]), (qb, pb))
    return jnp.moveaxis(out, 0, 1).reshape(out.shape[1], T, *out.shape[3:])


def _chunked_scan(step, carry, xs):
    T = xs[0].shape[1]
    L = CHUNK if T % CHUNK == 0 else T
    n = T // L
    xs_c = tuple(jnp.moveaxis(a.reshape(a.shape[0], n, L, *a.shape[2:]), 1, 0) for a in xs)
    carry, ys = lax.scan(step, carry, xs_c)
    return carry, jnp.moveaxis(ys, 0, 1).reshape(ys.shape[1], T, *ys.shape[3:])


def _diff_attn(q1, q2, q_pos, k1, k2, v, k_pos, table, lam):
    scale = DH_A ** -0.5
    bias = _t5_bias(q_pos, k_pos, table)[None]
    mask = (k_pos[None, :] <= q_pos[:, None])[None, None]

    def probs(q, k):
        s = jnp.einsum('bqhd,bkhd->bhqk', q, k).astype(jnp.float32) * scale + bias
        return jax.nn.softmax(jnp.where(mask, s, NEG), axis=-1)

    p = probs(q1, k1) - lam * probs(q2, k2)
    return jnp.einsum('bhqk,bkhe->bqhe', p.astype(v.dtype), v)


def _fox_attn(q, fq, q_pos, k, v, fk, k_pos):
    s = jnp.einsum('bqhd,bkhd->bhqk', q, k).astype(jnp.float32) * (DH_C ** -0.5)
    s = s + jnp.transpose(fq, (0, 2, 1))[..., None] - jnp.transpose(fk, (0, 2, 1))[:, :, None, :]
    mask = (k_pos[None, :] <= q_pos[:, None])[None, None]
    p = jax.nn.softmax(jnp.where(mask, s, NEG), axis=-1)
    return jnp.einsum('bhqk,bkhd->bqhd', p.astype(v.dtype), v)


def _hgrn2_chunk(S, xs):
    q, k, i, logf = xs
    L = q.shape[1]
    b = jnp.cumsum(logf, axis=1)
    causal = jnp.tril(jnp.ones((L, L), dtype=bool))
    o_inter = jnp.einsum('blhk,bhkv->blhv', q * jnp.exp(b), S)
    diff = b[:, :, None] - b[:, None, :]
    decay = jnp.exp(jnp.where(causal[None, :, :, None, None], diff, NEG))
    A = jnp.einsum('bthk,bshk,btshk->bhts', q, k, decay)
    o_intra = jnp.einsum('bhts,bshv->bthv', A, i)
    b_last = b[:, -1]
    k_dec = k * jnp.exp(b_last[:, None] - b)
    S_new = jnp.exp(b_last)[..., None] * S + jnp.einsum('bshk,bshv->bhkv', k_dec, i)
    return S_new, o_inter + o_intra


def _mlstm_chunk(carry, xs):
    C, n, m = carry
    q, k, v, ig, logf = xs
    L = q.shape[1]
    b = jnp.cumsum(logf, axis=1)
    causal = jnp.tril(jnp.ones((L, L), dtype=bool))
    D = b[:, :, None, :] - b[:, None, :, :] + ig[:, None, :, :]
    D = jnp.where(causal[None, :, :, None], D, NEG)
    inter = b + m[:, None, :]
    m_t = jnp.maximum(inter, jnp.max(D, axis=2))
    w_inter = jnp.exp(inter - m_t)
    W = jnp.exp(D - m_t[:, :, None, :])
    qk = jnp.einsum('bthk,bshk->btsh', q, k) * W
    num = w_inter[..., None] * jnp.einsum('bthk,bhkv->bthv', q, C) + jnp.einsum('btsh,bshv->bthv', qk, v)
    den = w_inter * jnp.einsum('bthk,bhk->bth', q, n) + jnp.sum(qk, axis=2)
    h = num / jnp.maximum(jnp.abs(den), jnp.exp(-m_t))[..., None]
    m_new = m_t[:, -1]
    w_c = jnp.exp(b[:, -1] + m - m_new)
    w_s = jnp.exp(b[:, -1:, :] - b + ig - m_new[:, None, :])
    C_new = w_c[..., None, None] * C + jnp.einsum('bsh,bshk,bshv->bhkv', w_s, k, v)
    n_new = w_c[..., None] * n + jnp.einsum('bsh,bshk->bhk', w_s, k)
    return (C_new, n_new, m_new), h


def _even_mixer(h, pos, past_k, past_v, s0, layer, w_in, lq1, lk1, lq2, lk2, subln, rel_bias, lb, gnorm, w_out):
    f32 = jnp.float32
    B, T, _ = h.shape
    qa, ka, va, qb, fb, ib, gb = _split(h @ w_in, SPLIT_EVEN)
    qa = qa.reshape(B, T, H_A, 2, DH_A)
    k_rows = ka.reshape(B, T, H_A, 2 * DH_A)
    v_rows = va.reshape(B, T, H_A, 2 * DH_A)
    if past_k is None:
        k_all, v_all, k_pos = k_rows, v_rows, pos
    else:
        k_all = jnp.concatenate([past_k.astype(k_rows.dtype), k_rows], axis=1)
        v_all = jnp.concatenate([past_v.astype(v_rows.dtype), v_rows], axis=1)
        k_pos = jnp.concatenate([jnp.arange(past_k.shape[1], dtype=jnp.int32), pos])
    k1, k2 = k_all[..., :DH_A], k_all[..., DH_A:]
    lam_init = 0.8 - 0.6 * math.exp(-0.3 * layer)
    lam = (jnp.exp(jnp.sum(lq1.astype(f32) * lk1.astype(f32)))
           - jnp.exp(jnp.sum(lq2.astype(f32) * lk2.astype(f32))) + lam_init)
    attn = _sweep_blocks(lambda a1, a2, qp: _diff_attn(a1, a2, qp, k1, k2, v_all, k_pos, rel_bias, lam),
                         (qa[..., 0, :], qa[..., 1, :]), pos)
    o_a = (_rmsnorm(attn, subln) * (1.0 - lam_init)).reshape(B, T, W_A).astype(h.dtype)
    f = lb + (1.0 - lb) * jax.nn.sigmoid(fb.astype(f32))
    shp = (B, T, H_B, DK_B)
    q = qb.astype(f32).reshape(shp) * (DK_B ** -0.5)
    k = (1.0 - f).reshape(shp)
    logf = jnp.log(f).reshape(shp)
    i = ib.astype(f32).reshape(B, T, H_B, DV_B)
    s_new, o = _chunked_scan(_hgrn2_chunk, s0.astype(f32), (q, k, i, logf))
    o_b = (_rmsnorm(o, gnorm) * jax.nn.silu(gb.astype(f32).reshape(B, T, H_B, DV_B))).reshape(B, T, W_B).astype(h.dtype)
    y = jnp.concatenate([o_a, o_b], axis=-1) @ w_out
    return y, k_rows, v_rows, s_new


def _odd_mixer(h, pos, past_k, past_v, past_lf, c0, n0, m0, w_in, bf_c, bi_d, bf_d, gnorm, w_out):
    f32 = jnp.float32
    B, T, _ = h.shape
    qc, kc, vc, fc, qd, kd, vd, id_, fd, od = _split(h @ w_in, SPLIT_ODD)
    q_c = qc.reshape(B, T, H_C, DH_C)
    k_rows = kc.reshape(B, T, H_C, DH_C)
    v_rows = vc.reshape(B, T, H_C, DH_C)
    lf_rows = jax.nn.log_sigmoid(fc.astype(f32) + bf_c.astype(f32))
    if past_k is None:
        k_all, v_all, k_pos = k_rows, v_rows, pos
        f_q = jnp.cumsum(lf_rows, axis=1)
        f_k = f_q
    else:
        k_all = jnp.concatenate([past_k.astype(k_rows.dtype), k_rows], axis=1)
        v_all = jnp.concatenate([past_v.astype(v_rows.dtype), v_rows], axis=1)
        k_pos = jnp.concatenate([jnp.arange(past_k.shape[1], dtype=jnp.int32), pos])
        f_past = jnp.cumsum(past_lf.astype(f32), axis=1)
        f_q = f_past[:, -1:] + jnp.cumsum(lf_rows, axis=1)
        f_k = jnp.concatenate([f_past, f_q], axis=1)
    o_c = _sweep_blocks(lambda qq, fq, qp: _fox_attn(qq, fq, qp, k_all, v_all, f_k, k_pos),
                        (q_c, f_q), pos).reshape(B, T, W_C)
    q_d = qd.astype(f32).reshape(B, T, H_D, DK_D)
    k_d = kd.astype(f32).reshape(B, T, H_D, DK_D) * (DK_D ** -0.5)
    v_d = vd.astype(f32).reshape(B, T, H_D, DV_D)
    ig = id_.astype(f32) + bi_d.astype(f32)
    lf_d = jax.nn.log_sigmoid(fd.astype(f32) + bf_d.astype(f32))
    (c_new, n_new, m_new), hd = _chunked_scan(
        _mlstm_chunk, (c0.astype(f32), n0.astype(f32), m0.astype(f32)), (q_d, k_d, v_d, ig, lf_d))
    o_d = (_rmsnorm(hd, gnorm) * jax.nn.sigmoid(od.astype(f32).reshape(B, T, H_D, DV_D))).reshape(B, T, W_D).astype(h.dtype)
    y = jnp.concatenate([o_c.astype(h.dtype), o_d], axis=-1) @ w_out
    return y, k_rows, v_rows, lf_rows, c_new, n_new, m_new


def _trunk(x, pos, past, params):
    (norm_mix, w_in_even, lambda_q1, lambda_k1, lambda_q2, lambda_k2, subln_a, rel_bias, lb_param,
     gnorm_b, w_out_even, w_in_odd, b_f_c, b_i_d, b_f_d, gnorm_d, w_out_odd,
     norm_mlp, w_up, w_down, norm_final) = params
    f32 = jnp.float32
    B = x.shape[0]
    lb_all = jnp.cumsum(jax.nn.softmax(lb_param.astype(f32), axis=0), axis=0)
    ka_l, va_l, sb_l, kc_l, vc_l, lfc_l, cd_l, nd_l, md_l = [], [], [], [], [], [], [], [], []
    for layer in range(DEPTH):
        j = layer // 2
        h = _rmsnorm(x, norm_mix[layer])
        if layer % 2 == 0:
            if past is None:
                pk, pv = None, None
                s0 = jnp.zeros((B, H_B, DK_B, DV_B), f32)
            else:
                page_table, c_ka, c_va, s_sb = past[0], past[1], past[2], past[3]
                pk = _gather_pages(c_ka[j], page_table)
                pv = _gather_pages(c_va[j], page_table)
                s0 = s_sb[j]
            y, kr, vr, s_new = _even_mixer(h, pos, pk, pv, s0, layer, w_in_even[j], lambda_q1[j], lambda_k1[j],
                                           lambda_q2[j], lambda_k2[j], subln_a[j], rel_bias, lb_all[layer],
                                           gnorm_b[j], w_out_even[j])
            ka_l.append(kr)
            va_l.append(vr)
            sb_l.append(s_new)
        else:
            if past is None:
                pk, pv, plf = None, None, None
                c0 = jnp.zeros((B, H_D, DK_D, DV_D), f32)
                n0 = jnp.zeros((B, H_D, DK_D), f32)
                m0 = jnp.zeros((B, H_D), f32)
            else:
                page_table, c_kc, c_vc, c_lfc = past[0], past[4], past[5], past[6]
                pk = _gather_pages(c_kc[j], page_table)
                pv = _gather_pages(c_vc[j], page_table)
                plf = _gather_pages(c_lfc[j], page_table)
                c0, n0, m0 = past[7][j], past[8][j], past[9][j]
            y, kr, vr, lfr, c_new, n_new, m_new = _odd_mixer(h, pos, pk, pv, plf, c0, n0, m0, w_in_odd[j],
                                                             b_f_c[j], b_i_d[j], b_f_d[j], gnorm_d[j], w_out_odd[j])
            kc_l.append(kr)
            vc_l.append(vr)
            lfc_l.append(lfr)
            cd_l.append(c_new)
            nd_l.append(n_new)
            md_l.append(m_new)
        x = x + y
        h = _rmsnorm(x, norm_mlp[layer])
        x = x + jnp.square(jax.nn.relu(h @ w_up[layer])) @ w_down[layer]
    out = _rmsnorm(x, norm_final)
    return out, (jnp.stack(ka_l), jnp.stack(va_l), jnp.stack(sb_l), jnp.stack(kc_l), jnp.stack(vc_l),
                 jnp.stack(lfc_l), jnp.stack(cd_l), jnp.stack(nd_l), jnp.stack(md_l))


def setup_inputs(seed: int = 0) -> dict:
    key = jax.random.key(seed)
    ks = iter(jax.random.split(key, 48))

    def nrm(shape, scale):
        return scale * jax.random.normal(next(ks), shape, jnp.float32)

    n_pages = PAST_LEN // PAGE_SIZE
    n_pool = (DEC_BATCH * n_pages * 5) // 4
    page_table = jax.random.permutation(next(ks), n_pool)[: DEC_BATCH * n_pages].reshape(DEC_BATCH, n_pages).astype(jnp.int32)
    return {
        'x_prompt': nrm((BATCH, SEQ, D_MODEL), 1.0),
        'x_sample': nrm((DEC_BATCH, DEC_SEQ, D_MODEL), 1.0),
        'cache_k_a': nrm((N_EVEN, n_pool, PAGE_SIZE, H_A, 2 * DH_A), 1.0),
        'cache_v_a': nrm((N_EVEN, n_pool, PAGE_SIZE, H_A, 2 * DH_A), 1.0),
        'state_s_b': nrm((N_EVEN, DEC_BATCH, H_B, DK_B, DV_B), 0.5),
        'cache_k_c': nrm((N_ODD, n_pool, PAGE_SIZE, H_C, DH_C), 1.0),
        'cache_v_c': nrm((N_ODD, n_pool, PAGE_SIZE, H_C, DH_C), 1.0),
        'cache_logf_c': jax.nn.log_sigmoid(2.0 + nrm((N_ODD, n_pool, PAGE_SIZE, H_C), 1.0)),
        'state_c_d': nrm((N_ODD, DEC_BATCH, H_D, DK_D, DV_D), 0.5),
        'state_n_d': nrm((N_ODD, DEC_BATCH, H_D, DK_D), 0.5),
        'state_m_d': nrm((N_ODD, DEC_BATCH, H_D), 1.0),
        'page_table': page_table,
        'norm_mix': 1.0 + nrm((DEPTH, D_MODEL), 0.02),
        'w_in_even': nrm((N_EVEN, D_MODEL, D_IN_EVEN), D_MODEL ** -0.5),
        'lambda_q1': nrm((N_EVEN, DH_A), 0.1),
        'lambda_k1': nrm((N_EVEN, DH_A), 0.1),
        'lambda_q2': nrm((N_EVEN, DH_A), 0.1),
        'lambda_k2': nrm((N_EVEN, DH_A), 0.1),
        'subln_a': 1.0 + nrm((N_EVEN, 2 * DH_A), 0.02),
        'rel_bias': nrm((NUM_BUCKETS, H_A), 0.5),
        'lb_param': nrm((DEPTH + 1, H_B * DK_B), 0.5),
        'gnorm_b': 1.0 + nrm((N_EVEN, DV_B), 0.02),
        'w_out_even': nrm((N_EVEN, W_A + W_B, D_MODEL), (W_A + W_B) ** -0.5),
        'w_in_odd': nrm((N_ODD, D_MODEL, D_IN_ODD), D_MODEL ** -0.5),
        'b_f_c': 2.0 + nrm((N_ODD, H_C), 0.1),
        'b_i_d': nrm((N_ODD, H_D), 0.1),
        'b_f_d': 3.0 + nrm((N_ODD, H_D), 0.5),
        'gnorm_d': 1.0 + nrm((N_ODD, DV_D), 0.02),
        'w_out_odd': nrm((N_ODD, W_C + W_D, D_MODEL), (W_C + W_D) ** -0.5),
        'norm_mlp': 1.0 + nrm((DEPTH, D_MODEL), 0.02),
        'w_up': nrm((DEPTH, D_MODEL, D_FF), D_MODEL ** -0.5),
        'w_down': nrm((DEPTH, D_FF, D_MODEL), D_FF ** -0.5),
        'norm_final': 1.0 + nrm((D_MODEL,), 0.02),
    }


def reference(x_prompt, x_sample, cache_k_a, cache_v_a, state_s_b, cache_k_c, cache_v_c, cache_logf_c,
              state_c_d, state_n_d, state_m_d, page_table, norm_mix, w_in_even, lambda_q1, lambda_k1,
              lambda_q2, lambda_k2, subln_a, rel_bias, lb_param, gnorm_b, w_out_even, w_in_odd, b_f_c,
              b_i_d, b_f_d, gnorm_d, w_out_odd, norm_mlp, w_up, w_down, norm_final):
    params = (norm_mix, w_in_even, lambda_q1, lambda_k1, lambda_q2, lambda_k2, subln_a, rel_bias, lb_param,
              gnorm_b, w_out_even, w_in_odd, b_f_c, b_i_d, b_f_d, gnorm_d, w_out_odd,
              norm_mlp, w_up, w_down, norm_final)
    past_len = page_table.shape[1] * cache_k_a.shape[2]
    pos_p = jnp.arange(x_prompt.shape[1], dtype=jnp.int32)
    pos_s = past_len + jnp.arange(x_sample.shape[1], dtype=jnp.int32)
    past = (page_table, cache_k_a, cache_v_a, state_s_b, cache_k_c, cache_v_c, cache_logf_c,
            state_c_d, state_n_d, state_m_d)
    y_prompt, st_p = _trunk(x_prompt, pos_p, None, params)
    y_sample, st_s = _trunk(x_sample, pos_s, past, params)
    p_k_a, p_v_a, p_s_b, p_k_c, p_v_c, p_lf_c, p_c_d, p_n_d, p_m_d = st_p
    s_k_a, s_v_a, s_s_b, s_k_c, s_v_c, s_lf_c, s_c_d, s_n_d, s_m_d = st_s
    return (y_prompt, y_sample, p_k_a, p_v_a, p_s_b, p_k_c, p_v_c, p_lf_c, p_c_d, p_n_d, p_m_d,
            s_k_a, s_v_a, s_s_b, s_k_c, s_v_c, s_lf_c, s_c_d, s_n_d, s_m_d)
```

```python
import functools
import math

import jax
import jax.numpy as jnp
from jax import lax
from jax.experimental import pallas as pl
from jax.experimental.pallas import tpu as pltpu

F32 = jnp.float32
BF16 = jnp.bfloat16
EPS = 1e-6
NEG = -1e30

LANES = 128
HALF = 64
NUM_BUCKETS = 32
MAX_EXACT = NUM_BUCKETS // 2
MAX_DISTANCE = 128
CHUNK = 64
SUB = 16
VMEM_LIMIT_BYTES = 56 * 1024 * 1024


def _params(*sem):
    return pltpu.CompilerParams(dimension_semantics=sem, vmem_limit_bytes=VMEM_LIMIT_BYTES)


def _dot(a, b):
    return jnp.dot(a, b, preferred_element_type=F32)


def _dot_nt(a, b):
    return lax.dot_general(a, b, (((1,), (1,)), ((), ())), preferred_element_type=F32)


def _dot_tn(a, b):
    return lax.dot_general(a, b, (((0,), (0,)), ((), ())), preferred_element_type=F32)


def _split3(x):
    hi = x.astype(BF16)
    r = x - hi.astype(F32)
    mid = r.astype(BF16)
    lo = (r - mid.astype(F32)).astype(BF16)
    return hi, mid, lo


def _exact_dot(onesmat, x):
    hi, mid, lo = _split3(x)
    return _dot(onesmat, hi) + _dot(onesmat, mid) + _dot(onesmat, lo)


def _exact_dot_tn(x, onesmat):
    hi, mid, lo = _split3(x)
    return _dot_tn(hi, onesmat) + _dot_tn(mid, onesmat) + _dot_tn(lo, onesmat)


def _exact_dot_r(x, onesmat):
    hi, mid, lo = _split3(x)
    return _dot(hi, onesmat) + _dot(mid, onesmat) + _dot(lo, onesmat)


def _tri(n, lower):
    r = lax.broadcasted_iota(jnp.int32, (n, n), 0)
    c = lax.broadcasted_iota(jnp.int32, (n, n), 1)
    keep = (c <= r) if lower else (r <= c)
    return jnp.where(keep, 1.0, 0.0).astype(BF16)


def _eye(n):
    r = lax.broadcasted_iota(jnp.int32, (n, n), 0)
    c = lax.broadcasted_iota(jnp.int32, (n, n), 1)
    return jnp.where(r == c, 1.0, 0.0).astype(BF16)


def _log_sigmoid(x):
    return jnp.minimum(x, 0.0) - jnp.log1p(jnp.exp(-jnp.abs(x)))


def _sigmoid(x):
    return 1.0 / (1.0 + jnp.exp(-x))


def _rms(x, g):
    return x * lax.rsqrt(jnp.mean(x * x, axis=-1, keepdims=True) + EPS) * g


def _norm_mm_kernel(x_ref, g_ref, w_ref, *rest, n_out, act):
    o_refs, h_scr = rest[:n_out], rest[n_out]
    j = pl.program_id(1)

    @pl.when(j == 0)
    def _():
        h_scr[...] = _rms(x_ref[...], g_ref[...]).astype(BF16)

    z = _dot(h_scr[...], w_ref[...])
    if act:
        z = jnp.square(jnp.maximum(z, 0.0))
    if n_out == 1:
        o_refs[0][...] = z.astype(o_refs[0].dtype)
    else:
        for idx in range(n_out):
            @pl.when(j == idx)
            def _(idx=idx):
                o_refs[idx][...] = z.astype(o_refs[idx].dtype)


def _norm_matmul(x, g, w, *, tm, tn, split, act=False, out_dtype=F32):
    m, d = x.shape
    n = w.shape[1]
    nj = n // tn
    n_out = nj if split else 1
    if split:
        out_shape = [jax.ShapeDtypeStruct((m, tn), out_dtype)] * nj
        out_specs = [pl.BlockSpec((tm, tn), lambda i, j: (i, 0))] * nj
    else:
        out_shape = [jax.ShapeDtypeStruct((m, n), out_dtype)]
        out_specs = [pl.BlockSpec((tm, tn), lambda i, j: (i, j))]
    outs = pl.pallas_call(
        functools.partial(_norm_mm_kernel, n_out=n_out, act=act),
        grid=(m // tm, nj),
        in_specs=[pl.BlockSpec((tm, d), lambda i, j: (i, 0)),
                  pl.BlockSpec((1, d), lambda i, j: (0, 0)),
                  pl.BlockSpec((d, tn), lambda i, j: (0, j))],
        out_specs=out_specs,
        out_shape=out_shape,
        scratch_shapes=[pltpu.VMEM((tm, d), BF16)],
        compiler_params=_params("parallel", "arbitrary"),
        name="norm_matmul",
    )(x, g.reshape(1, d), w)
    return outs if split else outs[0]


def _proj_res_kernel(*refs, n_a, ks, final_norm):
    a_refs, w_ref, res_ref = refs[:n_a], refs[n_a], refs[n_a + 1]
    o_ref = refs[-1]
    acc = res_ref[...]
    off = 0
    for a_ref, k in zip(a_refs, ks):
        acc = acc + _dot(a_ref[...], w_ref[off:off + k, :])
        off += k
    if final_norm:
        acc = _rms(acc, refs[n_a + 2][...])
    o_ref[...] = acc


def _proj_res(a_list, w, res, *, tm, final_g=None):
    m, n = res.shape
    ks = tuple(a.shape[1] for a in a_list)
    in_specs = [pl.BlockSpec((tm, k), lambda i: (i, 0)) for k in ks]
    in_specs += [pl.BlockSpec(w.shape, lambda i: (0, 0)),
                 pl.BlockSpec((tm, n), lambda i: (i, 0))]
    args = list(a_list) + [w, res]
    if final_g is not None:
        in_specs.append(pl.BlockSpec((1, n), lambda i: (0, 0)))
        args.append(final_g.reshape(1, n))
    return pl.pallas_call(
        functools.partial(_proj_res_kernel, n_a=len(a_list), ks=ks, final_norm=final_g is not None),
        grid=(m // tm,),
        in_specs=in_specs,
        out_specs=pl.BlockSpec((tm, n), lambda i: (i, 0)),
        out_shape=jax.ShapeDtypeStruct((m, n), F32),
        compiler_params=_params("parallel"),
        name="proj_res",
    )(*args)


def _t5_bias_value(n, tab_ref, h):
    nf = jnp.maximum(n, 1).astype(F32)
    large = MAX_EXACT + (jnp.log(nf / MAX_EXACT) / math.log(MAX_DISTANCE / MAX_EXACT)
                         * (NUM_BUCKETS - MAX_EXACT)).astype(jnp.int32)
    large = jnp.minimum(large, NUM_BUCKETS - 1)
    bucket = jnp.where(n < MAX_EXACT, n, large)
    val = jnp.zeros(n.shape, F32)
    for b in range(NUM_BUCKETS):
        val = jnp.where(bucket == b, tab_ref[b, h], val)
    return val


def _bias_tiles_kernel(tab_ref, o_ref, *, t):
    h, d = pl.program_id(0), pl.program_id(1)
    i = lax.broadcasted_iota(jnp.int32, (t, t), 0)
    j = lax.broadcasted_iota(jnp.int32, (t, t), 1)
    o_ref[0, 0] = _t5_bias_value(jnp.maximum(d * t + i - j, 0), tab_ref, h)


def _bias_tiles(rel_bias, *, t):
    nh = rel_bias.shape[1]
    return pl.pallas_call(
        functools.partial(_bias_tiles_kernel, t=t),
        grid=(nh, 2),
        in_specs=[pl.BlockSpec(memory_space=pltpu.SMEM)],
        out_specs=pl.BlockSpec((1, 1, t, t), lambda h, d: (h, d, 0, 0)),
        out_shape=jax.ShapeDtypeStruct((nh, 2, t, t), F32),
        compiler_params=_params("arbitrary", "arbitrary"),
        name="t5_bias_tiles",
    )(rel_bias)


def _lam_value(lq1, lk1, lq2, lk2, lam_init):
    s1 = jnp.sum(lq1[...] * lk1[...], axis=-1, keepdims=True)
    s2 = jnp.sum(lq2[...] * lk2[...], axis=-1, keepdims=True)
    return jnp.exp(s1) - jnp.exp(s2) + lam_init


def _flash2_kernel(*refs, mode, t, lam_init):
    if mode == "diff":
        (q_ref, k_ref, v_ref, bias_ref, tab_ref, lq1, lk1, lq2, lk2, subln_ref,
         o_ref, m_scr, l_scr, acc_scr) = refs
    else:
        q_ref, k_ref, v_ref, fq_ref, fk_ref, o_ref, m_scr, l_scr, acc_scr = refs
    hg, qi = pl.program_id(1), pl.program_id(2)
    lane = lax.broadcasted_iota(jnp.int32, (t, LANES), 1)
    q = q_ref[0] * (HALF ** -0.5)
    qs = (jnp.where(lane < HALF, q, 0.0).astype(BF16),
          jnp.where(lane >= HALF, q, 0.0).astype(BF16))
    m_scr[...] = jnp.full(m_scr.shape, NEG, F32)
    l_scr[...] = jnp.zeros(l_scr.shape, F32)
    acc_scr[...] = jnp.zeros(acc_scr.shape, F32)
    if mode == "fox":
        fq = fq_ref[0]
        fq_lane = lax.broadcasted_iota(jnp.int32, fq.shape, 1)
        fq_cols = [jnp.sum(jnp.where(fq_lane == 2 * hg + j, fq, 0.0), axis=-1, keepdims=True)
                   for j in range(2)]

    def step(kb, kind):
        start = pl.multiple_of(kb * t, t)
        k = k_ref[0, pl.ds(start, t), :].astype(BF16)
        v = v_ref[0, pl.ds(start, t), :].astype(BF16)
        for j in range(2):
            s = _dot_nt(qs[j], k)
            if mode == "diff":
                if kind == "far":
                    s = s + tab_ref[NUM_BUCKETS - 1, hg]
                else:
                    s = s + bias_ref[0, 0 if kind == "diag" else 1]
            else:
                fk = fk_ref[0, :, pl.ds(start, t)]
                fk_row = jnp.sum(jnp.where(lax.broadcasted_iota(jnp.int32, fk.shape, 0) == 2 * hg + j, fk, 0.0),
                                 axis=0, keepdims=True)
                s = s + (fq_cols[j] - fk_row)
            if kind == "diag":
                r = lax.broadcasted_iota(jnp.int32, (t, t), 0)
                c = lax.broadcasted_iota(jnp.int32, (t, t), 1)
                s = jnp.where(c <= r, s, NEG)
            m_prev = m_scr[j]
            m_new = jnp.maximum(m_prev, jnp.max(s, axis=-1, keepdims=True))
            alpha = jnp.exp(m_prev - m_new)
            p = jnp.exp(s - m_new)
            l_scr[j] = alpha * l_scr[j] + jnp.sum(p, axis=-1, keepdims=True)
            acc_scr[j] = alpha * acc_scr[j] + _dot(p.astype(BF16), v)
            m_scr[j] = m_new

    if mode == "diff":
        def far_body(kb, carry):
            step(kb, "far")
            return carry
        lax.fori_loop(0, qi - 1, far_body, 0)

        @pl.when(qi >= 1)
        def _():
            step(qi - 1, "sub")
    else:
        def far_body(kb, carry):
            step(kb, "far")
            return carry
        lax.fori_loop(0, qi, far_body, 0)
    step(qi, "diag")

    o0 = acc_scr[0] / l_scr[0]
    o1 = acc_scr[1] / l_scr[1]
    if mode == "diff":
        lam = _lam_value(lq1, lk1, lq2, lk2, lam_init)
        o = o0 - lam * o1
        o_ref[0] = (_rms(o, subln_ref[...]) * (1.0 - lam_init)).astype(o_ref.dtype)
    else:
        o_ref[0] = jnp.where(lane < HALF, o0, o1).astype(o_ref.dtype)


def _flash2(mode, q, k, v, extra, *, t, lam_init=0.0):
    b, tt, w = q.shape
    ng = w // LANES
    qspec = pl.BlockSpec((1, t, LANES), lambda bi, g, qi: (bi, qi, g))
    kvspec = pl.BlockSpec((1, tt, LANES), lambda bi, g, qi: (bi, 0, g))
    small = lambda shape: pl.BlockSpec(shape, lambda bi, g, qi: (0,) * len(shape))
    if mode == "diff":
        bias, tab, lq1, lk1, lq2, lk2, subln = extra
        in_specs = [qspec, kvspec, kvspec,
                    pl.BlockSpec((1, 2, t, t), lambda bi, g, qi: (g, 0, 0, 0)),
                    pl.BlockSpec(memory_space=pltpu.SMEM),
                    small((1, HALF)), small((1, HALF)), small((1, HALF)), small((1, HALF)),
                    small((1, LANES))]
        args = (q, k, v, bias, tab, lq1, lk1, lq2, lk2, subln)
    else:
        fcol, frow = extra
        nh = fcol.shape[-1]
        in_specs = [qspec, kvspec, kvspec,
                    pl.BlockSpec((1, t, nh), lambda bi, g, qi: (bi, qi, 0)),
                    pl.BlockSpec((1, nh, tt), lambda bi, g, qi: (bi, 0, 0))]
        args = (q, k, v, fcol, frow)
    return pl.pallas_call(
        functools.partial(_flash2_kernel, mode=mode, t=t, lam_init=lam_init),
        grid=(b, ng, tt // t),
        in_specs=in_specs,
        out_specs=qspec,
        out_shape=jax.ShapeDtypeStruct((b, tt, w), BF16),
        scratch_shapes=[pltpu.VMEM((2, t, 1), F32), pltpu.VMEM((2, t, 1), F32),
                        pltpu.VMEM((2, t, LANES), F32)],
        compiler_params=_params("parallel", "parallel", "arbitrary"),
        name="flash2_" + mode,
    )(*args)


def _fox_gates_kernel(g_ref, bias_ref, lf_ref, fcol_ref, frow_ref, crow_scr, ccol_scr, *, tc, nh):
    @pl.when(pl.program_id(1) == 0)
    def _():
        crow_scr[...] = jnp.zeros(crow_scr.shape, F32)
        ccol_scr[...] = jnp.zeros(ccol_scr.shape, F32)

    lf = _log_sigmoid(g_ref[0] + bias_ref[...])
    lf_ref[0] = lf[:, :nh]
    fcol = _exact_dot(_tri(tc, True), lf) + crow_scr[...]
    frow = _exact_dot_tn(lf, _tri(tc, False)) + ccol_scr[...]
    fcol_ref[0] = fcol[:, :nh]
    frow_ref[0] = frow[:nh, :]
    crow_scr[...] = fcol[tc - 1:tc, :]
    ccol_scr[...] = frow[:, tc - 1:tc]


def _fox_gates(gates, bias_row, *, tc, nh):
    b, tt, _ = gates.shape
    return pl.pallas_call(
        functools.partial(_fox_gates_kernel, tc=tc, nh=nh),
        grid=(b, tt // tc),
        in_specs=[pl.BlockSpec((1, tc, LANES), lambda bi, c: (bi, c, 0)),
                  pl.BlockSpec((1, LANES), lambda bi, c: (0, 0))],
        out_specs=[pl.BlockSpec((1, tc, nh), lambda bi, c: (bi, c, 0)),
                   pl.BlockSpec((1, tc, nh), lambda bi, c: (bi, c, 0)),
                   pl.BlockSpec((1, nh, tc), lambda bi, c: (bi, 0, c))],
        out_shape=[jax.ShapeDtypeStruct((b, tt, nh), F32),
                   jax.ShapeDtypeStruct((b, tt, nh), F32),
                   jax.ShapeDtypeStruct((b, nh, tt), F32)],
        scratch_shapes=[pltpu.VMEM((1, LANES), F32), pltpu.VMEM((LANES, 1), F32)],
        compiler_params=_params("parallel", "arbitrary"),
        name="fox_gates",
    )(gates, bias_row)


def _hgrn2_kernel(*refs, layer, n_layers, length, n_chunks, t_valid, has_state):
    if has_state:
        q_ref, f_ref, i_ref, g_ref, lbp_ref, gn_ref, s0_ref, o_ref, s_ref, st_scr = refs
    else:
        q_ref, f_ref, i_ref, g_ref, lbp_ref, gn_ref, o_ref, s_ref, st_scr = refs
    tb = pl.program_id(2)

    @pl.when(tb == 0)
    def _():
        if has_state:
            st_scr[...] = s0_ref[0, 0].T
        else:
            st_scr[...] = jnp.zeros(st_scr.shape, F32)

    lbp = lbp_ref[...]
    e = jnp.exp(lbp - jnp.max(lbp, axis=0, keepdims=True))
    sm = e / jnp.sum(e, axis=0, keepdims=True)
    lb = jnp.sum(sm[:layer + 1], axis=0, keepdims=True)

    L = length
    C = min(SUB, L)
    tril = _tri(L, True)
    row_c = lax.broadcasted_iota(jnp.int32, (C, LANES), 0)
    for c in range(n_chunks):
        rows = slice(c * L, (c + 1) * L)
        f = lb + (1.0 - lb) * _sigmoid(f_ref[0, rows, :])
        logf = jnp.log(f)
        kk = 1.0 - f
        if t_valid is not None:
            valid = lax.broadcasted_iota(jnp.int32, (L, LANES), 0) < t_valid
            logf = jnp.where(valid, logf, 0.0)
            kk = jnp.where(valid, kk, 0.0)
        q = q_ref[0, rows, :] * (LANES ** -0.5)
        iv = i_ref[0, rows, :]
        bcum = _exact_dot(tril, logf)
        st = st_scr[...]
        o_inter = _dot_nt((q * jnp.exp(bcum)).astype(BF16), st.astype(BF16))
        for sc in range(L // C):
            lo = sc * C
            q_s, b_s = q[lo:lo + C], bcum[lo:lo + C]
            o_s = o_inter[lo:lo + C]
            if sc > 0:
                ref_row = bcum[lo - 1:lo]
                qt = q_s * jnp.exp(b_s - ref_row)
                kt = kk[:lo] * jnp.exp(ref_row - bcum[:lo])
                a = _dot_nt(qt.astype(BF16), kt.astype(BF16))
                o_s = o_s + _dot(a.astype(BF16), iv[:lo].astype(BF16))
            for s in range(C):
                dec = jnp.where(row_c >= s, jnp.exp(jnp.minimum(b_s - bcum[lo + s:lo + s + 1], 0.0)), 0.0)
                a_col = jnp.sum(q_s * kk[lo + s:lo + s + 1] * dec, axis=-1, keepdims=True)
                o_s = o_s + a_col * iv[lo + s:lo + s + 1]
            gate = g_ref[0, c * L + lo:c * L + lo + C, :]
            o_ref[0, c * L + lo:c * L + lo + C, :] = (
                _rms(o_s, gn_ref[...]) * (gate * _sigmoid(gate))).astype(o_ref.dtype)
        b_last = bcum[L - 1:L]
        kd = kk * jnp.exp(b_last - bcum)
        st_scr[...] = st * jnp.exp(b_last) + _dot_tn(iv.astype(BF16), kd.astype(BF16))

    @pl.when(tb == pl.num_programs(2) - 1)
    def _():
        s_ref[0, 0] = st_scr[...].T


def _hgrn2(q, f, i, g, lb_param, gnorm, s0, *, layer, length, n_chunks, t_valid=None):
    b, tt, w = q.shape
    nh = w // LANES
    tb = length * n_chunks
    xspec = pl.BlockSpec((1, tb, LANES), lambda bi, h, ti: (bi, ti, h))
    sspec = pl.BlockSpec((1, 1, LANES, LANES), lambda bi, h, ti: (bi, h, 0, 0))
    nl = lb_param.shape[0]
    in_specs = [xspec, xspec, xspec, xspec,
                pl.BlockSpec((nl, LANES), lambda bi, h, ti: (0, h)),
                pl.BlockSpec((1, LANES), lambda bi, h, ti: (0, 0))]
    args = [q, f, i, g, lb_param, gnorm.reshape(1, LANES)]
    if s0 is not None:
        in_specs.append(sspec)
        args.append(s0)
    return pl.pallas_call(
        functools.partial(_hgrn2_kernel, layer=layer, n_layers=nl - 1, length=length,
                          n_chunks=n_chunks, t_valid=t_valid, has_state=s0 is not None),
        grid=(b, nh, tt // tb),
        in_specs=in_specs,
        out_specs=[xspec, sspec],
        out_shape=[jax.ShapeDtypeStruct((b, tt, w), BF16),
                   jax.ShapeDtypeStruct((b, nh, LANES, LANES), F32)],
        scratch_shapes=[pltpu.VMEM((LANES, LANES), F32)],
        compiler_params=_params("parallel", "parallel", "arbitrary"),
        name="hgrn2",
    )(*args)


def _mlstm_kernel(*refs, length, n_chunks, t_valid, has_state, i_lane0, f_lane0):
    if has_state:
        (q_ref, k_ref, v_ref, og_ref, gt_ref, gb_ref, gn_ref, c0_ref, n0_ref, m0_ref,
         o_ref, c_ref, n_ref, m_ref, c_scr, n_scr, m_scr) = refs
    else:
        (q_ref, k_ref, v_ref, og_ref, gt_ref, gb_ref, gn_ref,
         o_ref, c_ref, n_ref, m_ref, c_scr, n_scr, m_scr) = refs
    hp, tb = pl.program_id(1), pl.program_id(2)

    @pl.when(tb == 0)
    def _():
        if has_state:
            c_scr[...] = c0_ref[0, 0]
            n_scr[...] = n0_ref[0, 0]
            m_scr[...] = m0_ref[0, 0]
        else:
            c_scr[...] = jnp.zeros(c_scr.shape, F32)
            n_scr[...] = jnp.zeros(n_scr.shape, F32)
            m_scr[...] = jnp.zeros(m_scr.shape, F32)

    L = length
    tril, triu, eye = _tri(L, True), _tri(L, False), _eye(L)
    lane = lax.broadcasted_iota(jnp.int32, (L, LANES), 1)
    row_l = lax.broadcasted_iota(jnp.int32, (L, L), 0)
    col_l = lax.broadcasted_iota(jnp.int32, (L, L), 1)
    crow = lax.broadcasted_iota(jnp.int32, (LANES, LANES), 0)
    nlane = lax.broadcasted_iota(jnp.int32, (1, LANES), 1)
    is_f = (lane >= f_lane0) & (lane < f_lane0 + 4)
    for c in range(n_chunks):
        rows = slice(c * L, (c + 1) * L)
        gpre = gt_ref[0, rows, :] + gb_ref[...]
        x = jnp.where(is_f, _log_sigmoid(gpre), gpre)
        if t_valid is not None:
            valid = lax.broadcasted_iota(jnp.int32, (L, LANES), 0) < t_valid
            x = jnp.where(valid, x, jnp.where(is_f, 0.0, NEG))
        bcol_all = _exact_dot(tril, x)
        xt = _exact_dot_tn(x, eye)
        brow_all = _exact_dot_tn(x, triu)
        q = q_ref[0, rows, :]
        k = k_ref[0, rows, :] * (HALF ** -0.5)
        cst, nst = c_scr[...], n_scr[...]
        wc_rows = jnp.zeros((LANES, 1), F32)
        wc_lanes = jnp.zeros((1, LANES), F32)
        upd_c = jnp.zeros((LANES, LANES), F32)
        upd_n = jnp.zeros((1, LANES), F32)
        for j in range(2):
            h = 2 * hp + j
            sel = (lane >= j * HALF) & (lane < (j + 1) * HALF)
            qh = jnp.where(sel, q, 0.0)
            kh = jnp.where(sel, k, 0.0)
            vh = v_ref[0, rows, j * LANES:(j + 1) * LANES]
            pick = lambda a, l0: jnp.sum(jnp.where(lane == l0 + h, a, 0.0), axis=-1, keepdims=True)
            pick_t = lambda a, l0: jnp.sum(
                jnp.where(lax.broadcasted_iota(jnp.int32, a.shape, 0) == l0 + h, a, 0.0), axis=0, keepdims=True)
            b_col, ig_col = pick(bcol_all, f_lane0), pick(x, i_lane0)
            b_row, ig_row = pick_t(brow_all, f_lane0), pick_t(xt, i_lane0)
            m_old = jnp.sum(jnp.where(nlane == j, m_scr[...], 0.0), axis=-1, keepdims=True)
            d = jnp.where(col_l <= row_l, b_col - b_row + ig_row, NEG)
            inter = b_col + m_old
            m_t = jnp.maximum(inter, jnp.max(d, axis=-1, keepdims=True))
            w_inter = jnp.exp(inter - m_t)
            wmat = jnp.exp(d - m_t)
            qk = _dot_nt(qh.astype(BF16), kh.astype(BF16)) * wmat
            num = w_inter * _dot(qh.astype(BF16), cst.astype(BF16)) + _dot(qk.astype(BF16), vh.astype(BF16))
            den = (w_inter * jnp.sum(qh * nst, axis=-1, keepdims=True)
                   + jnp.sum(qk, axis=-1, keepdims=True))
            hd = num / jnp.maximum(jnp.abs(den), jnp.exp(-m_t))
            og = og_ref[0, rows, j * LANES:(j + 1) * LANES]
            o_ref[0, rows, j * LANES:(j + 1) * LANES] = (
                _rms(hd, gn_ref[...]) * _sigmoid(og)).astype(o_ref.dtype)
            m_new = m_t[L - 1:L]
            b_last = b_col[L - 1:L]
            w_c = jnp.exp(b_last + m_old - m_new)
            w_s = jnp.exp(b_last - b_col + ig_col - m_new)
            ks = kh * w_s
            upd_c = upd_c + _dot_tn(ks.astype(BF16), vh.astype(BF16))
            upd_n = upd_n + jnp.sum(ks, axis=0, keepdims=True)
            in_rows = (crow >= j * HALF) & (crow < (j + 1) * HALF)
            wc_rows = wc_rows + jnp.where(in_rows[:, :1], w_c, 0.0)
            in_lanes = (nlane >= j * HALF) & (nlane < (j + 1) * HALF)
            wc_lanes = wc_lanes + jnp.where(in_lanes, w_c, 0.0)
            m_scr[...] = jnp.where(nlane == j, m_new, m_scr[...])
        c_scr[...] = wc_rows * cst + upd_c
        n_scr[...] = wc_lanes * nst + upd_n

    @pl.when(tb == pl.num_programs(2) - 1)
    def _():
        c_ref[0, 0] = c_scr[...]
        n_ref[0, 0] = n_scr[...]
        m_ref[0, 0] = m_scr[...]


def _mlstm(qk, v, og, gates, gate_bias, gnorm, state, *, length, n_chunks, i_lane0, f_lane0, t_valid=None):
    b, tt, _ = qk.shape
    tb = length * n_chunks
    npair = 2
    qspec = pl.BlockSpec((1, tb, LANES), lambda bi, hp, ti: (bi, ti, hp))
    kspec = pl.BlockSpec((1, tb, LANES), lambda bi, hp, ti: (bi, ti, npair + hp))
    vspec = pl.BlockSpec((1, tb, 2 * LANES), lambda bi, hp, ti: (bi, ti, hp))
    gspec = pl.BlockSpec((1, tb, LANES), lambda bi, hp, ti: (bi, ti, 0))
    row = pl.BlockSpec((1, LANES), lambda bi, hp, ti: (0, 0))
    cspec = pl.BlockSpec((1, 1, LANES, LANES), lambda bi, hp, ti: (bi, hp, 0, 0))
    nspec = pl.BlockSpec((1, 1, 1, LANES), lambda bi, hp, ti: (bi, hp, 0, 0))
    in_specs = [qspec, kspec, vspec, vspec, gspec, row, row]
    args = [qk, qk, v, og, gates, gate_bias, gnorm.reshape(1, LANES)]
    if state is not None:
        in_specs += [cspec, nspec, nspec]
        args += list(state)
    return pl.pallas_call(
        functools.partial(_mlstm_kernel, length=length, n_chunks=n_chunks, t_valid=t_valid,
                          has_state=state is not None, i_lane0=i_lane0, f_lane0=f_lane0),
        grid=(b, npair, tt // tb),
        in_specs=in_specs,
        out_specs=[vspec, cspec, nspec, nspec],
        out_shape=[jax.ShapeDtypeStruct((b, tt, 4 * LANES), BF16),
                   jax.ShapeDtypeStruct((b, npair, LANES, LANES), F32),
                   jax.ShapeDtypeStruct((b, npair, 1, LANES), F32),
                   jax.ShapeDtypeStruct((b, npair, 1, LANES), F32)],
        scratch_shapes=[pltpu.VMEM((LANES, LANES), F32), pltpu.VMEM((1, LANES), F32),
                        pltpu.VMEM((1, LANES), F32)],
        compiler_params=_params("parallel", "parallel", "arbitrary"),
        name="mlstm",
    )(*args)


ROWS = 16


def _pad_rows(x, rows):
    return jnp.concatenate([x, jnp.zeros((rows - x.shape[0], x.shape[1]), x.dtype)], axis=0)


def _decode_diff_kernel(pt_ref, q_ref, kn_ref, vn_ref, tab_ref, lq1, lk1, lq2, lk2, subln_ref, *rest,
                        n_pages, page, n_heads, lam_init):
    del pt_ref
    k_refs, v_refs = rest[:n_pages], rest[n_pages:2 * n_pages]
    o_ref, bias_scr = rest[2 * n_pages], rest[2 * n_pages + 1]
    past = n_pages * page
    w = n_heads * LANES
    r_i = lax.broadcasted_iota(jnp.int32, (ROWS, w), 0)
    lane = lax.broadcasted_iota(jnp.int32, (ROWS, w), 1)
    row_head, row_map = r_i % 8, r_i // 8
    rcol = lax.broadcasted_iota(jnp.int32, (ROWS, 1), 0)

    @pl.when(pl.program_id(0) == 0)
    def _():
        n = past - lax.broadcasted_iota(jnp.int32, (1, past), 1)
        rr = lax.broadcasted_iota(jnp.int32, (ROWS, past), 0) % 8
        acc = jnp.zeros((ROWS, past), F32)
        for h in range(n_heads):
            acc = jnp.where(rr == h, _t5_bias_value(n, tab_ref, h), acc)
        bias_scr[...] = acc

    q = q_ref[0] * (HALF ** -0.5)
    qbig = jnp.where((lane // LANES == row_head) & ((lane % LANES) // HALF == row_map), q, 0.0)
    qb = qbig.astype(BF16)
    bias_new = jnp.zeros((ROWS, 1), F32)
    for h in range(n_heads):
        bias_new = jnp.where(rcol % 8 == h, tab_ref[0, h], bias_new)
    s_new = jnp.sum(qbig * kn_ref[0], axis=-1, keepdims=True) + bias_new
    s_pages = [_dot_nt(qb, k_refs[p][...].astype(BF16)) + bias_scr[:, p * page:(p + 1) * page]
               for p in range(n_pages)]
    m = s_new
    for s in s_pages:
        m = jnp.maximum(m, jnp.max(s, axis=-1, keepdims=True))
    e_new = jnp.exp(s_new - m)
    e_pages = [jnp.exp(s - m) for s in s_pages]
    l = e_new
    for e in e_pages:
        l = l + jnp.sum(e, axis=-1, keepdims=True)
    lam = _lam_value(lq1, lk1, lq2, lk2, lam_init)
    inv = 1.0 / l
    comb = lambda a: a[0:8] - lam * a[8:16]
    out = comb(e_new * inv) * vn_ref[0]
    for p in range(n_pages):
        wgt = _pad_rows(comb(e_pages[p] * inv), ROWS).astype(BF16)
        out = out + _dot(wgt, v_refs[p][...].astype(BF16))[0:8]
    own = (lax.broadcasted_iota(jnp.int32, (8, w), 1) // LANES
           == lax.broadcasted_iota(jnp.int32, (8, w), 0))
    o_m = jnp.where(own, out, 0.0)
    ms = jnp.sum(o_m * o_m, axis=-1, keepdims=True) / LANES
    y = o_m * lax.rsqrt(ms + EPS) * subln_ref[...] * (1.0 - lam_init)
    o_ref[0] = jnp.sum(y, axis=0, keepdims=True).astype(o_ref.dtype)


def _decode_diff(page_table, q, k_new, v_new, k_pool, v_pool, rel_bias, lq1, lk1, lq2, lk2, subln_tiled,
                 *, lam_init):
    ns, n_pages = page_table.shape
    _, page, w = k_pool.shape
    n_heads = w // LANES
    row = pl.BlockSpec((1, 1, w), lambda b, pt: (b, 0, 0))
    small = lambda shape: pl.BlockSpec(shape, lambda b, pt: (0,) * len(shape))
    page_spec = lambda p: pl.BlockSpec((None, page, w), lambda b, pt: (pt[b, p], 0, 0))
    in_specs = [row, row, row, pl.BlockSpec(memory_space=pltpu.SMEM),
                small((1, HALF)), small((1, HALF)), small((1, HALF)), small((1, HALF)), small((1, w))]
    in_specs += [page_spec(p) for p in range(n_pages)] * 2
    grid_spec = pltpu.PrefetchScalarGridSpec(
        num_scalar_prefetch=1, grid=(ns,), in_specs=in_specs, out_specs=row,
        scratch_shapes=[pltpu.VMEM((ROWS, n_pages * page), F32)])
    return pl.pallas_call(
        functools.partial(_decode_diff_kernel, n_pages=n_pages, page=page, n_heads=n_heads, lam_init=lam_init),
        grid_spec=grid_spec,
        out_shape=jax.ShapeDtypeStruct((ns, 1, w), BF16),
        compiler_params=_params("arbitrary"),
        name="decode_diff",
    )(page_table, q, k_new, v_new, rel_bias, lq1, lk1, lq2, lk2, subln_tiled,
      *([k_pool] * n_pages), *([v_pool] * n_pages))


def _decode_fox_kernel(pt_ref, q_ref, kn_ref, vn_ref, g_ref, gb_ref, *rest, n_pages, page, n_heads):
    del pt_ref
    f_refs, k_refs, v_refs = rest[:n_pages], rest[n_pages:2 * n_pages], rest[2 * n_pages:3 * n_pages]
    o_ref, lf_ref = rest[3 * n_pages], rest[3 * n_pages + 1]
    w = n_heads * HALF
    r_i = lax.broadcasted_iota(jnp.int32, (ROWS, w), 0)
    lane = lax.broadcasted_iota(jnp.int32, (ROWS, w), 1)
    q = q_ref[0] * (HALF ** -0.5)
    qbig = jnp.where(lane // HALF == r_i, q, 0.0)
    qb = qbig.astype(BF16)
    lf_row = _log_sigmoid(g_ref[0] + gb_ref[...])
    lf_ref[0] = lf_row[:, :n_heads]
    gr = lax.broadcasted_iota(jnp.int32, (ROWS, LANES), 0)
    gl = lax.broadcasted_iota(jnp.int32, (ROWS, LANES), 1)
    lf_col = jnp.sum(jnp.where((gl == gr) & (gr < n_heads), lf_row, 0.0), axis=-1, keepdims=True)
    triu = _tri(page, False)
    carry = jnp.zeros((ROWS, 1), F32)
    prefix = []
    for p in range(n_pages):
        pp = _exact_dot_r(_pad_rows(f_refs[p][...], ROWS), triu) + carry
        prefix.append(pp)
        carry = pp[:, page - 1:page]
    s_new = jnp.sum(qbig * kn_ref[0], axis=-1, keepdims=True)
    s_pages = [_dot_nt(qb, k_refs[p][...].astype(BF16)) + (lf_col + carry - prefix[p])
               for p in range(n_pages)]
    m = s_new
    for s in s_pages:
        m = jnp.maximum(m, jnp.max(s, axis=-1, keepdims=True))
    e_new = jnp.exp(s_new - m)
    e_pages = [jnp.exp(s - m) for s in s_pages]
    l = e_new
    for e in e_pages:
        l = l + jnp.sum(e, axis=-1, keepdims=True)
    inv = 1.0 / l
    out = (e_new * inv) * vn_ref[0]
    for p in range(n_pages):
        out = out + _dot((e_pages[p] * inv).astype(BF16), v_refs[p][...].astype(BF16))
    o_ref[0] = jnp.sum(jnp.where(lane // HALF == r_i, out, 0.0), axis=0, keepdims=True).astype(o_ref.dtype)


def _decode_fox(page_table, q, k_new, v_new, gates, gate_bias, lf_pool_t, k_pool, v_pool):
    ns, n_pages = page_table.shape
    _, page, w = k_pool.shape
    n_heads = w // HALF
    row = pl.BlockSpec((1, 1, w), lambda b, pt: (b, 0, 0))
    grow = pl.BlockSpec((1, 1, LANES), lambda b, pt: (b, 0, 0))
    page_spec = lambda p: pl.BlockSpec((None, page, w), lambda b, pt: (pt[b, p], 0, 0))
    f_spec = lambda p: pl.BlockSpec((None, n_heads, page), lambda b, pt: (pt[b, p], 0, 0))
    in_specs = [row, row, row, grow, pl.BlockSpec((1, LANES), lambda b, pt: (0, 0))]
    in_specs += [f_spec(p) for p in range(n_pages)] + [page_spec(p) for p in range(n_pages)] * 2
    grid_spec = pltpu.PrefetchScalarGridSpec(
        num_scalar_prefetch=1, grid=(ns,), in_specs=in_specs,
        out_specs=[row, pl.BlockSpec((1, 1, n_heads), lambda b, pt: (b, 0, 0))])
    return pl.pallas_call(
        functools.partial(_decode_fox_kernel, n_pages=n_pages, page=page, n_heads=n_heads),
        grid_spec=grid_spec,
        out_shape=[jax.ShapeDtypeStruct((ns, 1, w), BF16), jax.ShapeDtypeStruct((ns, 1, n_heads), F32)],
        compiler_params=_params("arbitrary"),
        name="decode_fox",
    )(page_table, q, k_new, v_new, gates, gate_bias,
      *([lf_pool_t] * n_pages), *([k_pool] * n_pages), *([v_pool] * n_pages))


def _mlp(x, g, w_up, w_down, *, tm, final_g=None):
    mid = _norm_matmul(x, g, w_up, tm=tm, tn=512, split=False, act=True, out_dtype=BF16)
    return _proj_res([mid], w_down, x, tm=tm, final_g=final_g)


def kernel(x_prompt, x_sample, cache_k_a, cache_v_a, state_s_b, cache_k_c, cache_v_c, cache_logf_c,
           state_c_d, state_n_d, state_m_d, page_table, norm_mix, w_in_even, lambda_q1, lambda_k1,
           lambda_q2, lambda_k2, subln_a, rel_bias, lb_param, gnorm_b, w_out_even, w_in_odd, b_f_c,
           b_i_d, b_f_d, gnorm_d, w_out_odd, norm_mlp, w_up, w_down, norm_final):
    B, T, D = x_prompt.shape
    S = x_sample.shape[0]
    n_pool, page = cache_k_a.shape[1], cache_k_a.shape[2]
    h_a, h_c, h_d = cache_k_a.shape[3], cache_k_c.shape[3], state_c_d.shape[2]
    w_a = h_a * LANES
    lam_init = 0.8 - 0.6 * math.exp(-0.3 * 0)
    tq = 512

    w_in0 = w_in_even[0].astype(BF16)
    wo = w_in_odd[0]
    c = [0]
    for sz in (h_c * HALF, h_c * HALF, h_c * HALF, h_c, h_d * HALF, h_d * HALF, h_d * LANES, h_d, h_d,
               h_d * LANES):
        c.append(c[-1] + sz)
    col = lambda i: wo[:, c[i]:c[i + 1]]
    gate_cols = jnp.concatenate([col(3), col(7), col(8)], axis=1)
    gate_cols = jnp.pad(gate_cols, ((0, 0), (0, w_a - gate_cols.shape[1])))
    w_in1 = jnp.concatenate([col(0), col(1), col(2), col(4), col(5), col(6), col(9), gate_cols],
                            axis=1).astype(BF16)
    gate_bias = jnp.pad(jnp.concatenate([b_f_c[0], b_i_d[0], b_f_d[0]]), (0, LANES - h_c - 2 * h_d))
    gate_bias = gate_bias.reshape(1, LANES)
    i_lane0, f_lane0 = h_c, h_c + h_d
    w_out0, w_out1 = w_out_even[0].astype(BF16), w_out_odd[0].astype(BF16)
    w_up_b, w_down_b = w_up.astype(BF16), w_down.astype(BF16)
    lq1, lk1, lq2, lk2 = (a.reshape(1, HALF) for a in (lambda_q1[0], lambda_k1[0], lambda_q2[0], lambda_k2[0]))
    subln = subln_a[0].reshape(1, LANES)

    M = B * T
    xp = x_prompt.reshape(M, D)
    qa, ka, va, qb, fb, ib, gb = _norm_matmul(xp, norm_mix[0], w_in0, tm=512, tn=w_a, split=True)
    r3 = lambda a: a.reshape(B, T, w_a)
    bias = _bias_tiles(rel_bias, t=tq)
    o_a = _flash2("diff", r3(qa), r3(ka), r3(va), (bias, rel_bias, lq1, lk1, lq2, lk2, subln),
                  t=tq, lam_init=lam_init)
    o_b, p_s_b = _hgrn2(r3(qb), r3(fb), r3(ib), r3(gb), lb_param, gnorm_b[0], None,
                        layer=0, length=CHUNK, n_chunks=4)
    x1 = _proj_res([o_a.reshape(M, w_a), o_b.reshape(M, w_a)], w_out0, xp, tm=512)
    x2 = _mlp(x1, norm_mlp[0], w_up_b[0], w_down_b[0], tm=512)
    qc, kc, vc, qkd, vd, od, gts = _norm_matmul(x2, norm_mix[1], w_in1, tm=512, tn=w_a, split=True)
    p_lf_c, fcol, frow = _fox_gates(r3(gts), gate_bias, tc=512, nh=h_c)
    o_c = _flash2("fox", r3(qc), r3(kc), r3(vc), (fcol, frow), t=tq)
    o_d, p_c, p_n, p_m = _mlstm(r3(qkd), r3(vd), r3(od), r3(gts), gate_bias, gnorm_d[0], None,
                                length=CHUNK, n_chunks=4, i_lane0=i_lane0, f_lane0=f_lane0)
    x3 = _proj_res([o_c.reshape(M, w_a), o_d.reshape(M, w_a)], w_out1, x2, tm=512)
    y_prompt = _mlp(x3, norm_mlp[1], w_up_b[1], w_down_b[1], tm=512, final_g=norm_final).reshape(B, T, D)

    PADT = 8
    xs = x_sample.reshape(S, D)
    sqa, ska, sva, sqb, sfb, sib, sgb = _norm_matmul(xs, norm_mix[0], w_in0, tm=S, tn=w_a, split=True)
    s3 = lambda a: a.reshape(S, 1, w_a)
    padt = lambda a: jnp.pad(a.reshape(S, 1, w_a), ((0, 0), (0, PADT - 1), (0, 0)))
    so_a = _decode_diff(page_table, s3(sqa), s3(ska), s3(sva),
                        cache_k_a[0].reshape(n_pool, page, w_a), cache_v_a[0].reshape(n_pool, page, w_a),
                        rel_bias, lq1, lk1, lq2, lk2, jnp.tile(subln, (1, h_a)), lam_init=lam_init)
    so_b, s_s_b = _hgrn2(padt(sqb), padt(sfb), padt(sib), padt(sgb), lb_param, gnorm_b[0], state_s_b[0],
                         layer=0, length=PADT, n_chunks=1, t_valid=1)
    sx1 = _proj_res([so_a.reshape(S, w_a), so_b[:, 0]], w_out0, xs, tm=S)
    sx2 = _mlp(sx1, norm_mlp[0], w_up_b[0], w_down_b[0], tm=S)
    sqc, skc, svc, sqkd, svd, sod, sgts = _norm_matmul(sx2, norm_mix[1], w_in1, tm=S, tn=w_a, split=True)
    lf_pool_t = jnp.swapaxes(cache_logf_c[0], 1, 2)
    so_c, s_lf_c = _decode_fox(page_table, s3(sqc), s3(skc), s3(svc), sgts[:, :LANES].reshape(S, 1, LANES),
                               gate_bias, lf_pool_t,
                               cache_k_c[0].reshape(n_pool, page, w_a), cache_v_c[0].reshape(n_pool, page, w_a))
    npair = h_d // 2
    m0 = jnp.pad(state_m_d[0].reshape(S, npair, 1, 2), ((0, 0), (0, 0), (0, 0), (0, LANES - 2)))
    so_d, s_c, s_n, s_m = _mlstm(padt(sqkd), padt(svd), padt(sod), padt(sgts), gate_bias, gnorm_d[0],
                                 (state_c_d[0].reshape(S, npair, LANES, LANES),
                                  state_n_d[0].reshape(S, npair, 1, LANES), m0),
                                 length=PADT, n_chunks=1, i_lane0=i_lane0, f_lane0=f_lane0, t_valid=1)
    sx3 = _proj_res([so_c.reshape(S, w_a), so_d[:, 0]], w_out1, sx2, tm=S)
    y_sample = _mlp(sx3, norm_mlp[1], w_up_b[1], w_down_b[1], tm=S, final_g=norm_final).reshape(S, 1, D)

    dk_d = state_c_d.shape[3]
    heads = lambda a, nb, tt, nh: a.reshape(1, nb, tt, nh, w_a // nh)
    unpair_c = lambda a, nb: a.reshape(1, nb, h_d, dk_d, LANES)
    unpair_n = lambda a, nb: a.reshape(1, nb, h_d, dk_d)
    unpair_m = lambda a, nb: a[..., :2].reshape(1, nb, h_d)
    return (y_prompt, y_sample,
            heads(ka, B, T, h_a), heads(va, B, T, h_a), p_s_b[None],
            heads(kc, B, T, h_c), heads(vc, B, T, h_c), p_lf_c[None],
            unpair_c(p_c, B), unpair_n(p_n, B), unpair_m(p_m, B),
            heads(ska, S, 1, h_a), heads(sva, S, 1, h_a), s_s_b[None],
            heads(skc, S, 1, h_c), heads(svc, S, 1, h_c), s_lf_c[None],
            unpair_c(s_c, S), unpair_n(s_n, S), unpair_m(s_m, S))
```

```python
import functools
import math

import jax
import jax.numpy as jnp
from jax import lax
from jax.experimental import pallas as pl
from jax.experimental.pallas import tpu as pltpu

F32 = jnp.float32
BF16 = jnp.bfloat16
EPS = 1e-6
NEG = -1e30

LANES = 128
HALF = 64
NUM_BUCKETS = 32
MAX_EXACT = NUM_BUCKETS // 2
MAX_DISTANCE = 128
CHUNK = 64
SUB = 16
VMEM_LIMIT_BYTES = 56 * 1024 * 1024


def _params(*sem):
    return pltpu.CompilerParams(dimension_semantics=sem, vmem_limit_bytes=VMEM_LIMIT_BYTES)


def _dot(a, b):
    return jnp.dot(a, b, preferred_element_type=F32)


def _dot_nt(a, b):
    return lax.dot_general(a, b, (((1,), (1,)), ((), ())), preferred_element_type=F32)


def _dot_tn(a, b):
    return lax.dot_general(a, b, (((0,), (0,)), ((), ())), preferred_element_type=F32)


def _split3(x):
    hi = x.astype(BF16)
    r = x - hi.astype(F32)
    mid = r.astype(BF16)
    lo = (r - mid.astype(F32)).astype(BF16)
    return hi, mid, lo


def _exact_dot(onesmat, x):
    hi, mid, lo = _split3(x)
    return _dot(onesmat, hi) + _dot(onesmat, mid) + _dot(onesmat, lo)


def _exact_dot_tn(x, onesmat):
    hi, mid, lo = _split3(x)
    return _dot_tn(hi, onesmat) + _dot_tn(mid, onesmat) + _dot_tn(lo, onesmat)


def _exact_dot_r(x, onesmat):
    hi, mid, lo = _split3(x)
    return _dot(hi, onesmat) + _dot(mid, onesmat) + _dot(lo, onesmat)


def _tri(n, lower):
    r = lax.broadcasted_iota(jnp.int32, (n, n), 0)
    c = lax.broadcasted_iota(jnp.int32, (n, n), 1)
    keep = (c <= r) if lower else (r <= c)
    return jnp.where(keep, 1.0, 0.0).astype(BF16)


def _eye(n):
    r = lax.broadcasted_iota(jnp.int32, (n, n), 0)
    c = lax.broadcasted_iota(jnp.int32, (n, n), 1)
    return jnp.where(r == c, 1.0, 0.0).astype(BF16)


def _log_sigmoid(x):
    return jnp.minimum(x, 0.0) - jnp.log1p(jnp.exp(-jnp.abs(x)))


def _sigmoid(x):
    return 1.0 / (1.0 + jnp.exp(-x))


def _rms(x, g):
    return x * lax.rsqrt(jnp.mean(x * x, axis=-1, keepdims=True) + EPS) * g


def _norm_mm_kernel(x_ref, g_ref, w_ref, *o_refs, tn):
    h = _rms(x_ref[...], g_ref[...]).astype(BF16)
    for j, o_ref in enumerate(o_refs):
        o_ref[...] = _dot(h, w_ref[:, j * tn:(j + 1) * tn])


def _norm_matmul(x, g, w, *, tm, tn):
    m, d = x.shape
    n = w.shape[1]
    nj = n // tn
    return pl.pallas_call(
        functools.partial(_norm_mm_kernel, tn=tn),
        grid=(m // tm,),
        in_specs=[pl.BlockSpec((tm, d), lambda i: (i, 0)),
                  pl.BlockSpec((1, d), lambda i: (0, 0)),
                  pl.BlockSpec((d, n), lambda i: (0, 0))],
        out_specs=[pl.BlockSpec((tm, tn), lambda i: (i, 0))] * nj,
        out_shape=[jax.ShapeDtypeStruct((m, tn), F32)] * nj,
        compiler_params=_params("parallel"),
        name="norm_matmul",
    )(x, g.reshape(1, d), w)


def _mlp_kernel(x_ref, g_ref, wu_ref, wd_ref, *rest, tf, final_norm):
    o_ref = rest[-1]
    x = x_ref[...]
    h = _rms(x, g_ref[...]).astype(BF16)
    acc = x
    for c in range(wu_ref.shape[1] // tf):
        u = _dot(h, wu_ref[:, c * tf:(c + 1) * tf])
        u = jnp.square(jnp.maximum(u, 0.0)).astype(BF16)
        acc = acc + _dot(u, wd_ref[c * tf:(c + 1) * tf, :])
    if final_norm:
        acc = _rms(acc, rest[0][...])
    o_ref[...] = acc


def _mlp(x, g, w_up, w_down, *, tm, tf=512, final_g=None):
    m, d = x.shape
    f = w_up.shape[1]
    in_specs = [pl.BlockSpec((tm, d), lambda i: (i, 0)),
                pl.BlockSpec((1, d), lambda i: (0, 0)),
                pl.BlockSpec((d, f), lambda i: (0, 0)),
                pl.BlockSpec((f, d), lambda i: (0, 0))]
    args = [x, g.reshape(1, d), w_up, w_down]
    if final_g is not None:
        in_specs.append(pl.BlockSpec((1, d), lambda i: (0, 0)))
        args.append(final_g.reshape(1, d))
    return pl.pallas_call(
        functools.partial(_mlp_kernel, tf=tf, final_norm=final_g is not None),
        grid=(m // tm,),
        in_specs=in_specs,
        out_specs=pl.BlockSpec((tm, d), lambda i: (i, 0)),
        out_shape=jax.ShapeDtypeStruct((m, d), F32),
        compiler_params=_params("parallel"),
        name="mlp",
    )(*args)


def _proj_res_kernel(*refs, n_a, ks):
    a_refs, w_ref, res_ref, o_ref = refs[:n_a], refs[n_a], refs[n_a + 1], refs[n_a + 2]
    acc = res_ref[...]
    off = 0
    for a_ref, k in zip(a_refs, ks):
        acc = acc + _dot(a_ref[...], w_ref[off:off + k, :])
        off += k
    o_ref[...] = acc


def _proj_res(a_list, w, res, *, tm):
    m, n = res.shape
    ks = tuple(a.shape[1] for a in a_list)
    in_specs = [pl.BlockSpec((tm, k), lambda i: (i, 0)) for k in ks]
    in_specs += [pl.BlockSpec(w.shape, lambda i: (0, 0)),
                 pl.BlockSpec((tm, n), lambda i: (i, 0))]
    return pl.pallas_call(
        functools.partial(_proj_res_kernel, n_a=len(a_list), ks=ks),
        grid=(m // tm,),
        in_specs=in_specs,
        out_specs=pl.BlockSpec((tm, n), lambda i: (i, 0)),
        out_shape=jax.ShapeDtypeStruct((m, n), F32),
        compiler_params=_params("parallel"),
        name="proj_res",
    )(*a_list, w, res)


def _t5_bias_value(n, tab_ref, h):
    nf = jnp.maximum(n, 1).astype(F32)
    large = MAX_EXACT + (jnp.log(nf / MAX_EXACT) / math.log(MAX_DISTANCE / MAX_EXACT)
                         * (NUM_BUCKETS - MAX_EXACT)).astype(jnp.int32)
    large = jnp.minimum(large, NUM_BUCKETS - 1)
    bucket = jnp.where(n < MAX_EXACT, n, large)
    val = jnp.zeros(n.shape, F32)
    for b in range(NUM_BUCKETS):
        val = jnp.where(bucket == b, tab_ref[b, h], val)
    return val


def _bias_tiles_kernel(tab_ref, o_ref, *, t):
    h, d = pl.program_id(0), pl.program_id(1)
    i = lax.broadcasted_iota(jnp.int32, (t, t), 0)
    j = lax.broadcasted_iota(jnp.int32, (t, t), 1)
    o_ref[0, 0] = _t5_bias_value(jnp.maximum(d * t + j - i, 0), tab_ref, h)


def _bias_tiles(rel_bias, *, t):
    nh = rel_bias.shape[1]
    return pl.pallas_call(
        functools.partial(_bias_tiles_kernel, t=t),
        grid=(nh, 2),
        in_specs=[pl.BlockSpec(memory_space=pltpu.SMEM)],
        out_specs=pl.BlockSpec((1, 1, t, t), lambda h, d: (h, d, 0, 0)),
        out_shape=jax.ShapeDtypeStruct((nh, 2, t, t), F32),
        compiler_params=_params("arbitrary", "arbitrary"),
        name="t5_bias_tiles",
    )(rel_bias)


def _lam_value(lq1, lk1, lq2, lk2, lam_init):
    s1 = jnp.sum(lq1[...] * lk1[...], axis=-1, keepdims=True)
    s2 = jnp.sum(lq2[...] * lk2[...], axis=-1, keepdims=True)
    return jnp.exp(s1) - jnp.exp(s2) + lam_init


def _flash2_kernel(*refs, mode, t, lam_init):
    if mode == "diff":
        (q_ref, k_ref, v_ref, bias_ref, tab_ref, lq1, lk1, lq2, lk2, subln_ref,
         o_ref, m_scr, l_scr, acc_scr, kb_scr, vt_scr) = refs
    else:
        q_ref, k_ref, v_ref, fq_ref, fk_ref, o_ref, m_scr, l_scr, acc_scr, kb_scr, vt_scr = refs
    hg, qi = pl.program_id(1), pl.program_id(2)
    n_blk = k_ref.shape[1] // t

    @pl.when(qi == 0)
    def _():
        for kb in range(n_blk):
            kb_scr[kb * t:(kb + 1) * t, :] = k_ref[0, kb * t:(kb + 1) * t, :].astype(BF16)
            vt_scr[:, kb * t:(kb + 1) * t] = v_ref[0, kb * t:(kb + 1) * t, :].T.astype(BF16)

    lane = lax.broadcasted_iota(jnp.int32, (t, LANES), 1)
    q = q_ref[0] * (HALF ** -0.5)
    qs = (jnp.where(lane < HALF, q, 0.0).astype(BF16),
          jnp.where(lane >= HALF, q, 0.0).astype(BF16))
    m_scr[...] = jnp.full(m_scr.shape, NEG, F32)
    l_scr[...] = jnp.zeros(l_scr.shape, F32)
    acc_scr[...] = jnp.zeros(acc_scr.shape, F32)
    if mode == "fox":
        fq = fq_ref[0]
        fq_row = lax.broadcasted_iota(jnp.int32, fq.shape, 0)
        fq_rows = [jnp.sum(jnp.where(fq_row == 2 * hg + j, fq, 0.0), axis=0, keepdims=True)
                   for j in range(2)]

    def step(kb, kind):
        start = pl.multiple_of(kb * t, t)
        k = kb_scr[pl.ds(start, t), :]
        vt = vt_scr[:, pl.ds(start, t)]
        for j in range(2):
            s = _dot_nt(k, qs[j])
            if mode == "diff":
                if kind == "far":
                    s = s + tab_ref[NUM_BUCKETS - 1, hg]
                else:
                    s = s + bias_ref[0, 0 if kind == "diag" else 1]
            else:
                fk = fk_ref[0, pl.ds(start, t), :]
                fk_col = jnp.sum(jnp.where(lax.broadcasted_iota(jnp.int32, fk.shape, 1) == 2 * hg + j, fk, 0.0),
                                 axis=-1, keepdims=True)
                s = s + (fq_rows[j] - fk_col)
            if kind == "diag":
                r = lax.broadcasted_iota(jnp.int32, (t, t), 0)
                c = lax.broadcasted_iota(jnp.int32, (t, t), 1)
                s = jnp.where(r <= c, s, NEG)
            m_prev = m_scr[j]
            m_new = jnp.maximum(m_prev, jnp.max(s, axis=0, keepdims=True))
            alpha = jnp.exp(m_prev - m_new)
            p = jnp.exp(s - m_new)
            l_scr[j] = alpha * l_scr[j] + jnp.sum(p, axis=0, keepdims=True)
            acc_scr[j] = alpha * acc_scr[j] + _dot(vt, p.astype(BF16))
            m_scr[j] = m_new

    if mode == "diff":
        def far_body(kb, carry):
            step(kb, "far")
            return carry
        lax.fori_loop(0, qi - 1, far_body, 0)

        @pl.when(qi >= 1)
        def _():
            step(qi - 1, "sub")
    else:
        def far_body(kb, carry):
            step(kb, "far")
            return carry
        lax.fori_loop(0, qi, far_body, 0)
    step(qi, "diag")

    o0 = acc_scr[0] / l_scr[0]
    o1 = acc_scr[1] / l_scr[1]
    if mode == "diff":
        lam = _lam_value(lq1, lk1, lq2, lk2, lam_init)
        o = (o0 - lam * o1).T
        o_ref[0] = (_rms(o, subln_ref[...]) * (1.0 - lam_init)).astype(o_ref.dtype)
    else:
        row = lax.broadcasted_iota(jnp.int32, (LANES, t), 0)
        o_ref[0] = jnp.where(row < HALF, o0, o1).T.astype(o_ref.dtype)


def _flash2(mode, q, k, v, extra, *, t, lam_init=0.0):
    b, tt, w = q.shape
    ng = w // LANES
    qspec = pl.BlockSpec((1, t, LANES), lambda bi, g, qi: (bi, qi, g))
    kvspec = pl.BlockSpec((1, tt, LANES), lambda bi, g, qi: (bi, 0, g))
    small = lambda shape: pl.BlockSpec(shape, lambda bi, g, qi: (0,) * len(shape))
    if mode == "diff":
        bias, tab, lq1, lk1, lq2, lk2, subln = extra
        in_specs = [qspec, kvspec, kvspec,
                    pl.BlockSpec((1, 2, t, t), lambda bi, g, qi: (g, 0, 0, 0)),
                    pl.BlockSpec(memory_space=pltpu.SMEM),
                    small((1, HALF)), small((1, HALF)), small((1, HALF)), small((1, HALF)),
                    small((1, LANES))]
        args = (q, k, v, bias, tab, lq1, lk1, lq2, lk2, subln)
    else:
        fcol, frow = extra
        nh = fcol.shape[-1]
        in_specs = [qspec, kvspec, kvspec,
                    pl.BlockSpec((1, nh, t), lambda bi, g, qi: (bi, 0, qi)),
                    pl.BlockSpec((1, tt, nh), lambda bi, g, qi: (bi, 0, 0))]
        args = (q, k, v, frow, fcol)
    return pl.pallas_call(
        functools.partial(_flash2_kernel, mode=mode, t=t, lam_init=lam_init),
        grid=(b, ng, tt // t),
        in_specs=in_specs,
        out_specs=qspec,
        out_shape=jax.ShapeDtypeStruct((b, tt, w), BF16),
        scratch_shapes=[pltpu.VMEM((2, 1, t), F32), pltpu.VMEM((2, 1, t), F32),
                        pltpu.VMEM((2, LANES, t), F32),
                        pltpu.VMEM((tt, LANES), BF16), pltpu.VMEM((LANES, tt), BF16)],
        compiler_params=_params("parallel", "parallel", "arbitrary"),
        name="flash2_" + mode,
    )(*args)


def _fox_gates_kernel(g_ref, bias_ref, lf_ref, fcol_ref, frow_ref, crow_scr, ccol_scr, *, tc, nh):
    @pl.when(pl.program_id(1) == 0)
    def _():
        crow_scr[...] = jnp.zeros(crow_scr.shape, F32)
        ccol_scr[...] = jnp.zeros(ccol_scr.shape, F32)

    lf = _log_sigmoid(g_ref[0] + bias_ref[...])
    lf_ref[0] = lf[:, :nh]
    fcol = _exact_dot(_tri(tc, True), lf) + crow_scr[...]
    frow = _exact_dot_tn(lf, _tri(tc, False)) + ccol_scr[...]
    fcol_ref[0] = fcol[:, :nh]
    frow_ref[0] = frow[:nh, :]
    crow_scr[...] = fcol[tc - 1:tc, :]
    ccol_scr[...] = frow[:, tc - 1:tc]


def _fox_gates(gates, bias_row, *, tc, nh):
    b, tt, _ = gates.shape
    return pl.pallas_call(
        functools.partial(_fox_gates_kernel, tc=tc, nh=nh),
        grid=(b, tt // tc),
        in_specs=[pl.BlockSpec((1, tc, LANES), lambda bi, c: (bi, c, 0)),
                  pl.BlockSpec((1, LANES), lambda bi, c: (0, 0))],
        out_specs=[pl.BlockSpec((1, tc, nh), lambda bi, c: (bi, c, 0)),
                   pl.BlockSpec((1, tc, nh), lambda bi, c: (bi, c, 0)),
                   pl.BlockSpec((1, nh, tc), lambda bi, c: (bi, 0, c))],
        out_shape=[jax.ShapeDtypeStruct((b, tt, nh), F32),
                   jax.ShapeDtypeStruct((b, tt, nh), F32),
                   jax.ShapeDtypeStruct((b, nh, tt), F32)],
        scratch_shapes=[pltpu.VMEM((1, LANES), F32), pltpu.VMEM((LANES, 1), F32)],
        compiler_params=_params("parallel", "arbitrary"),
        name="fox_gates",
    )(gates, bias_row)


def _hgrn2_kernel(*refs, layer, n_layers, length, n_chunks, bb, t_valid, has_state):
    if has_state:
        q_ref, f_ref, i_ref, g_ref, lbp_ref, gn_ref, s0_ref, o_ref, s_ref, st_scr = refs
    else:
        q_ref, f_ref, i_ref, g_ref, lbp_ref, gn_ref, o_ref, s_ref, st_scr = refs
    tb = pl.program_id(2)

    @pl.when(tb == 0)
    def _():
        for bi in range(bb):
            if has_state:
                st_scr[bi] = s0_ref[bi, 0].T
            else:
                st_scr[bi] = jnp.zeros(st_scr.shape[1:], F32)

    lbp = lbp_ref[...]
    e = jnp.exp(lbp - jnp.max(lbp, axis=0, keepdims=True))
    sm = e / jnp.sum(e, axis=0, keepdims=True)
    lb = jnp.sum(sm[:layer + 1], axis=0, keepdims=True)

    L = length
    C = min(SUB, L)
    tril = _tri(L, True)
    row_c = lax.broadcasted_iota(jnp.int32, (C, LANES), 0)
    for bi, c in [(bi, c) for bi in range(bb) for c in range(n_chunks)]:
        rows = slice(c * L, (c + 1) * L)
        f = lb + (1.0 - lb) * _sigmoid(f_ref[bi, rows, :])
        logf = jnp.log(f)
        kk = 1.0 - f
        if t_valid is not None:
            valid = lax.broadcasted_iota(jnp.int32, (L, LANES), 0) < t_valid
            logf = jnp.where(valid, logf, 0.0)
            kk = jnp.where(valid, kk, 0.0)
        q = q_ref[bi, rows, :] * (LANES ** -0.5)
        iv = i_ref[bi, rows, :]
        bcum = _exact_dot(tril, logf)
        st = st_scr[bi]
        o_inter = _dot_nt((q * jnp.exp(bcum)).astype(BF16), st.astype(BF16))
        for sc in range(L // C):
            lo = sc * C
            q_s, b_s = q[lo:lo + C], bcum[lo:lo + C]
            o_s = o_inter[lo:lo + C]
            if sc > 0:
                ref_row = bcum[lo - 1:lo]
                qt = q_s * jnp.exp(b_s - ref_row)
                kt = kk[:lo] * jnp.exp(ref_row - bcum[:lo])
                a = _dot_nt(qt.astype(BF16), kt.astype(BF16))
                o_s = o_s + _dot(a.astype(BF16), iv[:lo].astype(BF16))
            for s in range(C):
                dec = jnp.where(row_c >= s, jnp.exp(jnp.minimum(b_s - bcum[lo + s:lo + s + 1], 0.0)), 0.0)
                a_col = jnp.sum(q_s * kk[lo + s:lo + s + 1] * dec, axis=-1, keepdims=True)
                o_s = o_s + a_col * iv[lo + s:lo + s + 1]
            gate = g_ref[bi, c * L + lo:c * L + lo + C, :]
            o_ref[bi, c * L + lo:c * L + lo + C, :] = (
                _rms(o_s, gn_ref[...]) * (gate * _sigmoid(gate))).astype(o_ref.dtype)
        b_last = bcum[L - 1:L]
        kd = kk * jnp.exp(b_last - bcum)
        st_scr[bi] = st * jnp.exp(b_last) + _dot_tn(iv.astype(BF16), kd.astype(BF16))

    @pl.when(tb == pl.num_programs(2) - 1)
    def _():
        for bi in range(bb):
            s_ref[bi, 0] = st_scr[bi].T


def _hgrn2(q, f, i, g, lb_param, gnorm, s0, *, layer, length, n_chunks, bb=1, t_valid=None):
    b, tt, w = q.shape
    nh = w // LANES
    tb = length * n_chunks
    xspec = pl.BlockSpec((bb, tb, LANES), lambda bi, h, ti: (bi, ti, h))
    sspec = pl.BlockSpec((bb, 1, LANES, LANES), lambda bi, h, ti: (bi, h, 0, 0))
    nl = lb_param.shape[0]
    in_specs = [xspec, xspec, xspec, xspec,
                pl.BlockSpec((nl, LANES), lambda bi, h, ti: (0, h)),
                pl.BlockSpec((1, LANES), lambda bi, h, ti: (0, 0))]
    args = [q, f, i, g, lb_param, gnorm.reshape(1, LANES)]
    if s0 is not None:
        in_specs.append(sspec)
        args.append(s0)
    return pl.pallas_call(
        functools.partial(_hgrn2_kernel, layer=layer, n_layers=nl - 1, length=length,
                          n_chunks=n_chunks, bb=bb, t_valid=t_valid, has_state=s0 is not None),
        grid=(b // bb, nh, tt // tb),
        in_specs=in_specs,
        out_specs=[xspec, sspec],
        out_shape=[jax.ShapeDtypeStruct((b, tt, w), BF16),
                   jax.ShapeDtypeStruct((b, nh, LANES, LANES), F32)],
        scratch_shapes=[pltpu.VMEM((bb, LANES, LANES), F32)],
        compiler_params=_params("parallel", "parallel", "arbitrary"),
        name="hgrn2",
    )(*args)


def _mlstm_kernel(*refs, length, n_chunks, bb, t_valid, has_state, i_lane0, f_lane0):
    if has_state:
        (q_ref, k_ref, v_ref, og_ref, gt_ref, gb_ref, gn_ref, c0_ref, n0_ref, m0_ref,
         o_ref, c_ref, n_ref, m_ref, c_scr, n_scr, m_scr) = refs
    else:
        (q_ref, k_ref, v_ref, og_ref, gt_ref, gb_ref, gn_ref,
         o_ref, c_ref, n_ref, m_ref, c_scr, n_scr, m_scr) = refs
    hp, tb = pl.program_id(1), pl.program_id(2)

    @pl.when(tb == 0)
    def _():
        for bi in range(bb):
            if has_state:
                c_scr[bi] = c0_ref[bi, 0]
                n_scr[bi] = n0_ref[bi, 0]
                m_scr[bi] = m0_ref[bi, 0]
            else:
                c_scr[bi] = jnp.zeros(c_scr.shape[1:], F32)
                n_scr[bi] = jnp.zeros(n_scr.shape[1:], F32)
                m_scr[bi] = jnp.zeros(m_scr.shape[1:], F32)

    L = length
    tril, triu, eye = _tri(L, True), _tri(L, False), _eye(L)
    lane = lax.broadcasted_iota(jnp.int32, (L, LANES), 1)
    row_l = lax.broadcasted_iota(jnp.int32, (L, L), 0)
    col_l = lax.broadcasted_iota(jnp.int32, (L, L), 1)
    crow = lax.broadcasted_iota(jnp.int32, (LANES, LANES), 0)
    nlane = lax.broadcasted_iota(jnp.int32, (1, LANES), 1)
    is_f = (lane >= f_lane0) & (lane < f_lane0 + 4)
    for bi, c in [(bi, c) for bi in range(bb) for c in range(n_chunks)]:
        rows = slice(c * L, (c + 1) * L)
        gpre = gt_ref[bi, rows, :] + gb_ref[...]
        x = jnp.where(is_f, _log_sigmoid(gpre), gpre)
        if t_valid is not None:
            valid = lax.broadcasted_iota(jnp.int32, (L, LANES), 0) < t_valid
            x = jnp.where(valid, x, jnp.where(is_f, 0.0, NEG))
        bcol_all = _exact_dot(tril, x)
        xt = _exact_dot_tn(x, eye)
        brow_all = _exact_dot_tn(x, triu)
        q = q_ref[bi, rows, :]
        k = k_ref[bi, rows, :] * (HALF ** -0.5)
        cst, nst = c_scr[bi], n_scr[bi]
        wc_rows = jnp.zeros((LANES, 1), F32)
        wc_lanes = jnp.zeros((1, LANES), F32)
        upd_c = jnp.zeros((LANES, LANES), F32)
        upd_n = jnp.zeros((1, LANES), F32)
        for j in range(2):
            h = 2 * hp + j
            sel = (lane >= j * HALF) & (lane < (j + 1) * HALF)
            qh = jnp.where(sel, q, 0.0)
            kh = jnp.where(sel, k, 0.0)
            vh = v_ref[bi, rows, j * LANES:(j + 1) * LANES]
            pick = lambda a, l0: jnp.sum(jnp.where(lane == l0 + h, a, 0.0), axis=-1, keepdims=True)
            pick_t = lambda a, l0: jnp.sum(
                jnp.where(lax.broadcasted_iota(jnp.int32, a.shape, 0) == l0 + h, a, 0.0), axis=0, keepdims=True)
            b_col, ig_col = pick(bcol_all, f_lane0), pick(x, i_lane0)
            b_row, ig_row = pick_t(brow_all, f_lane0), pick_t(xt, i_lane0)
            m_old = jnp.sum(jnp.where(nlane == j, m_scr[bi], 0.0), axis=-1, keepdims=True)
            d = jnp.where(col_l <= row_l, b_col - b_row + ig_row, NEG)
            inter = b_col + m_old
            m_t = jnp.maximum(inter, jnp.max(d, axis=-1, keepdims=True))
            w_inter = jnp.exp(inter - m_t)
            wmat = jnp.exp(d - m_t)
            qk = _dot_nt(qh.astype(BF16), kh.astype(BF16)) * wmat
            num = w_inter * _dot(qh.astype(BF16), cst.astype(BF16)) + _dot(qk.astype(BF16), vh.astype(BF16))
            den = (w_inter * jnp.sum(qh * nst, axis=-1, keepdims=True)
                   + jnp.sum(qk, axis=-1, keepdims=True))
            hd = num / jnp.maximum(jnp.abs(den), jnp.exp(-m_t))
            og = og_ref[bi, rows, j * LANES:(j + 1) * LANES]
            o_ref[bi, rows, j * LANES:(j + 1) * LANES] = (
                _rms(hd, gn_ref[...]) * _sigmoid(og)).astype(o_ref.dtype)
            m_new = m_t[L - 1:L]
            b_last = b_col[L - 1:L]
            w_c = jnp.exp(b_last + m_old - m_new)
            w_s = jnp.exp(b_last - b_col + ig_col - m_new)
            ks = kh * w_s
            upd_c = upd_c + _dot_tn(ks.astype(BF16), vh.astype(BF16))
            upd_n = upd_n + jnp.sum(ks, axis=0, keepdims=True)
            in_rows = (crow >= j * HALF) & (crow < (j + 1) * HALF)
            wc_rows = wc_rows + jnp.where(in_rows[:, :1], w_c, 0.0)
            in_lanes = (nlane >= j * HALF) & (nlane < (j + 1) * HALF)
            wc_lanes = wc_lanes + jnp.where(in_lanes, w_c, 0.0)
            m_scr[bi] = jnp.where(nlane == j, m_new, m_scr[bi])
        c_scr[bi] = wc_rows * cst + upd_c
        n_scr[bi] = wc_lanes * nst + upd_n

    @pl.when(tb == pl.num_programs(2) - 1)
    def _():
        for bi in range(bb):
            c_ref[bi, 0] = c_scr[bi]
            n_ref[bi, 0] = n_scr[bi]
            m_ref[bi, 0] = m_scr[bi]


def _mlstm(qk, v, og, gates, gate_bias, gnorm, state, *, length, n_chunks, i_lane0, f_lane0, bb=1,
           t_valid=None):
    b, tt, _ = qk.shape
    tb = length * n_chunks
    npair = 2
    qspec = pl.BlockSpec((bb, tb, LANES), lambda bi, hp, ti: (bi, ti, hp))
    kspec = pl.BlockSpec((bb, tb, LANES), lambda bi, hp, ti: (bi, ti, npair + hp))
    vspec = pl.BlockSpec((bb, tb, 2 * LANES), lambda bi, hp, ti: (bi, ti, hp))
    gspec = pl.BlockSpec((bb, tb, LANES), lambda bi, hp, ti: (bi, ti, 0))
    row = pl.BlockSpec((1, LANES), lambda bi, hp, ti: (0, 0))
    cspec = pl.BlockSpec((bb, 1, LANES, LANES), lambda bi, hp, ti: (bi, hp, 0, 0))
    nspec = pl.BlockSpec((bb, 1, 1, LANES), lambda bi, hp, ti: (bi, hp, 0, 0))
    in_specs = [qspec, kspec, vspec, vspec, gspec, row, row]
    args = [qk, qk, v, og, gates, gate_bias, gnorm.reshape(1, LANES)]
    if state is not None:
        in_specs += [cspec, nspec, nspec]
        args += list(state)
    return pl.pallas_call(
        functools.partial(_mlstm_kernel, length=length, n_chunks=n_chunks, bb=bb, t_valid=t_valid,
                          has_state=state is not None, i_lane0=i_lane0, f_lane0=f_lane0),
        grid=(b // bb, npair, tt // tb),
        in_specs=in_specs,
        out_specs=[vspec, cspec, nspec, nspec],
        out_shape=[jax.ShapeDtypeStruct((b, tt, 4 * LANES), BF16),
                   jax.ShapeDtypeStruct((b, npair, LANES, LANES), F32),
                   jax.ShapeDtypeStruct((b, npair, 1, LANES), F32),
                   jax.ShapeDtypeStruct((b, npair, 1, LANES), F32)],
        scratch_shapes=[pltpu.VMEM((bb, LANES, LANES), F32), pltpu.VMEM((bb, 1, LANES), F32),
                        pltpu.VMEM((bb, 1, LANES), F32)],
        compiler_params=_params("parallel", "parallel", "arbitrary"),
        name="mlstm",
    )(*args)


ROWS = 16


def _pad_rows(x, rows):
    return jnp.concatenate([x, jnp.zeros((rows - x.shape[0], x.shape[1]), x.dtype)], axis=0)


def _decode_diff_kernel(pt_ref, q_ref, kn_ref, vn_ref, tab_ref, lq1, lk1, lq2, lk2, subln_ref, *rest,
                        n_pages, page, n_heads, lam_init):
    del pt_ref
    k_refs, v_refs = rest[:n_pages], rest[n_pages:2 * n_pages]
    o_ref, bias_scr = rest[2 * n_pages], rest[2 * n_pages + 1]
    past = n_pages * page
    pw = page * n_heads
    rr = lax.broadcasted_iota(jnp.int32, (ROWS, pw), 0) % 8
    ll = lax.broadcasted_iota(jnp.int32, (ROWS, pw), 1)
    own = (ll % n_heads == rr) & (rr < n_heads)
    rcol = lax.broadcasted_iota(jnp.int32, (ROWS, 1), 0) % 8

    @pl.when(pl.program_id(0) == 0)
    def _():
        for p in range(n_pages):
            n = past - (p * page + ll // n_heads)
            acc = jnp.zeros((ROWS, pw), F32)
            for h in range(n_heads):
                acc = jnp.where(rr == h, _t5_bias_value(n, tab_ref, h), acc)
            bias_scr[:, p * pw:(p + 1) * pw] = acc

    lane = lax.broadcasted_iota(jnp.int32, (8, LANES), 1)
    q8 = _pad_rows(q_ref[0], 8) * (HALF ** -0.5)
    qm = jnp.concatenate([jnp.where(lane < HALF, q8, 0.0), jnp.where(lane >= HALF, q8, 0.0)], axis=0)
    qb = qm.astype(BF16)
    kn8 = _pad_rows(kn_ref[0], 8)
    bias_new = jnp.zeros((ROWS, 1), F32)
    for h in range(n_heads):
        bias_new = jnp.where(rcol == h, tab_ref[0, h], bias_new)
    s_new = jnp.sum(qm * jnp.concatenate([kn8, kn8], axis=0), axis=-1, keepdims=True) + bias_new
    s_pages = [jnp.where(own, _dot_nt(qb, k_refs[p][...].astype(BF16)) + bias_scr[:, p * pw:(p + 1) * pw], NEG)
               for p in range(n_pages)]
    m = s_new
    for s in s_pages:
        m = jnp.maximum(m, jnp.max(s, axis=-1, keepdims=True))
    e_new = jnp.exp(s_new - m)
    e_pages = [jnp.exp(s - m) for s in s_pages]
    l = e_new
    for e in e_pages:
        l = l + jnp.sum(e, axis=-1, keepdims=True)
    lam = _lam_value(lq1, lk1, lq2, lk2, lam_init)
    inv = 1.0 / l
    comb = lambda a: a[0:8] - lam * a[8:16]
    out = comb(e_new * inv) * _pad_rows(vn_ref[0], 8)
    for p in range(n_pages):
        wgt = _pad_rows(comb(e_pages[p] * inv), ROWS).astype(BF16)
        out = out + _dot(wgt, v_refs[p][...].astype(BF16))[0:8]
    y = _rms(out, subln_ref[...]) * (1.0 - lam_init)
    o_ref[0] = y[0:n_heads].astype(o_ref.dtype)


def _decode_diff(page_table, q, k_new, v_new, k_pool, v_pool, rel_bias, lq1, lk1, lq2, lk2, subln, *, lam_init):
    ns, n_pages = page_table.shape
    n_heads = q.shape[1]
    pw = k_pool.shape[1]
    row = pl.BlockSpec((1, n_heads, LANES), lambda b, pt: (b, 0, 0))
    small = lambda shape: pl.BlockSpec(shape, lambda b, pt: (0,) * len(shape))
    page_spec = lambda p: pl.BlockSpec((None, pw, LANES), lambda b, pt: (pt[b, p], 0, 0))
    in_specs = [row, row, row, pl.BlockSpec(memory_space=pltpu.SMEM),
                small((1, HALF)), small((1, HALF)), small((1, HALF)), small((1, HALF)), small((1, LANES))]
    in_specs += [page_spec(p) for p in range(n_pages)] * 2
    grid_spec = pltpu.PrefetchScalarGridSpec(
        num_scalar_prefetch=1, grid=(ns,), in_specs=in_specs, out_specs=row,
        scratch_shapes=[pltpu.VMEM((ROWS, n_pages * pw), F32)])
    return pl.pallas_call(
        functools.partial(_decode_diff_kernel, n_pages=n_pages, page=pw // n_heads, n_heads=n_heads,
                          lam_init=lam_init),
        grid_spec=grid_spec,
        out_shape=jax.ShapeDtypeStruct((ns, n_heads, LANES), BF16),
        compiler_params=_params("arbitrary"),
        name="decode_diff",
    )(page_table, q, k_new, v_new, rel_bias, lq1, lk1, lq2, lk2, subln,
      *([k_pool] * n_pages), *([v_pool] * n_pages))


def _decode_fox_kernel(pt_ref, q_ref, kn_ref, vn_ref, g_ref, gb_ref, *rest, n_pages, page, n_heads):
    del pt_ref
    f_refs, k_refs, v_refs = rest[:n_pages], rest[n_pages:2 * n_pages], rest[2 * n_pages:3 * n_pages]
    o_ref, lf_ref = rest[3 * n_pages], rest[3 * n_pages + 1]
    w = n_heads * HALF
    r_i = lax.broadcasted_iota(jnp.int32, (ROWS, w), 0)
    lane = lax.broadcasted_iota(jnp.int32, (ROWS, w), 1)
    q = q_ref[0] * (HALF ** -0.5)
    qbig = jnp.where(lane // HALF == r_i, q, 0.0)
    qb = qbig.astype(BF16)
    lf_row = _log_sigmoid(g_ref[0] + gb_ref[...])
    lf_ref[0] = lf_row[:, :n_heads]
    gr = lax.broadcasted_iota(jnp.int32, (ROWS, LANES), 0)
    gl = lax.broadcasted_iota(jnp.int32, (ROWS, LANES), 1)
    lf_col = jnp.sum(jnp.where((gl == gr) & (gr < n_heads), lf_row, 0.0), axis=-1, keepdims=True)
    triu = _tri(page, False)
    carry = jnp.zeros((ROWS, 1), F32)
    prefix = []
    for p in range(n_pages):
        pp = _exact_dot_r(_pad_rows(f_refs[p][...], ROWS), triu) + carry
        prefix.append(pp)
        carry = pp[:, page - 1:page]
    s_new = jnp.sum(qbig * kn_ref[0], axis=-1, keepdims=True)
    s_pages = [_dot(qb, k_refs[p][...].astype(BF16)) + (lf_col + carry - prefix[p])
               for p in range(n_pages)]
    m = s_new
    for s in s_pages:
        m = jnp.maximum(m, jnp.max(s, axis=-1, keepdims=True))
    e_new = jnp.exp(s_new - m)
    e_pages = [jnp.exp(s - m) for s in s_pages]
    l = e_new
    for e in e_pages:
        l = l + jnp.sum(e, axis=-1, keepdims=True)
    inv = 1.0 / l
    out = (e_new * inv) * vn_ref[0]
    for p in range(n_pages):
        out = out + _dot_nt((e_pages[p] * inv).astype(BF16), v_refs[p][...].astype(BF16))
    o_ref[0] = jnp.sum(jnp.where(lane // HALF == r_i, out, 0.0), axis=0, keepdims=True).astype(o_ref.dtype)


def _decode_fox(page_table, q, k_new, v_new, gates, gate_bias, lf_pool_t, k_pool_t, v_pool_t):
    ns, n_pages = page_table.shape
    _, w, page = k_pool_t.shape
    n_heads = w // HALF
    row = pl.BlockSpec((1, 1, w), lambda b, pt: (b, 0, 0))
    grow = pl.BlockSpec((1, 1, LANES), lambda b, pt: (b, 0, 0))
    page_spec = lambda p: pl.BlockSpec((None, w, page), lambda b, pt: (pt[b, p], 0, 0))
    f_spec = lambda p: pl.BlockSpec((None, n_heads, page), lambda b, pt: (pt[b, p], 0, 0))
    in_specs = [row, row, row, grow, pl.BlockSpec((1, LANES), lambda b, pt: (0, 0))]
    in_specs += [f_spec(p) for p in range(n_pages)] + [page_spec(p) for p in range(n_pages)] * 2
    grid_spec = pltpu.PrefetchScalarGridSpec(
        num_scalar_prefetch=1, grid=(ns,), in_specs=in_specs,
        out_specs=[row, pl.BlockSpec((1, 1, n_heads), lambda b, pt: (b, 0, 0))])
    return pl.pallas_call(
        functools.partial(_decode_fox_kernel, n_pages=n_pages, page=page, n_heads=n_heads),
        grid_spec=grid_spec,
        out_shape=[jax.ShapeDtypeStruct((ns, 1, w), BF16), jax.ShapeDtypeStruct((ns, 1, n_heads), F32)],
        compiler_params=_params("arbitrary"),
        name="decode_fox",
    )(page_table, q, k_new, v_new, gates, gate_bias,
      *([lf_pool_t] * n_pages), *([k_pool_t] * n_pages), *([v_pool_t] * n_pages))


def kernel(x_prompt, x_sample, cache_k_a, cache_v_a, state_s_b, cache_k_c, cache_v_c, cache_logf_c,
           state_c_d, state_n_d, state_m_d, page_table, norm_mix, w_in_even, lambda_q1, lambda_k1,
           lambda_q2, lambda_k2, subln_a, rel_bias, lb_param, gnorm_b, w_out_even, w_in_odd, b_f_c,
           b_i_d, b_f_d, gnorm_d, w_out_odd, norm_mlp, w_up, w_down, norm_final):
    B, T, D = x_prompt.shape
    S = x_sample.shape[0]
    n_pool, page = cache_k_a.shape[1], cache_k_a.shape[2]
    h_a, h_c, h_d = cache_k_a.shape[3], cache_k_c.shape[3], state_c_d.shape[2]
    w_a = h_a * LANES
    lam_init = 0.8 - 0.6 * math.exp(-0.3 * 0)
    tq = 512

    w_in0 = w_in_even[0].astype(BF16)
    wo = w_in_odd[0]
    c = [0]
    for sz in (h_c * HALF, h_c * HALF, h_c * HALF, h_c, h_d * HALF, h_d * HALF, h_d * LANES, h_d, h_d,
               h_d * LANES):
        c.append(c[-1] + sz)
    col = lambda i: wo[:, c[i]:c[i + 1]]
    gate_cols = jnp.concatenate([col(3), col(7), col(8)], axis=1)
    gate_cols = jnp.pad(gate_cols, ((0, 0), (0, w_a - gate_cols.shape[1])))
    w_in1 = jnp.concatenate([col(0), col(1), col(2), col(4), col(5), col(6), col(9), gate_cols],
                            axis=1).astype(BF16)
    gate_bias = jnp.pad(jnp.concatenate([b_f_c[0], b_i_d[0], b_f_d[0]]), (0, LANES - h_c - 2 * h_d))
    gate_bias = gate_bias.reshape(1, LANES)
    i_lane0, f_lane0 = h_c, h_c + h_d
    w_out0, w_out1 = w_out_even[0].astype(BF16), w_out_odd[0].astype(BF16)
    w_up_b, w_down_b = w_up.astype(BF16), w_down.astype(BF16)
    lq1, lk1, lq2, lk2 = (a.reshape(1, HALF) for a in (lambda_q1[0], lambda_k1[0], lambda_q2[0], lambda_k2[0]))
    subln = subln_a[0].reshape(1, LANES)

    M = B * T
    xp = x_prompt.reshape(M, D)
    qa, ka, va, qb, fb, ib, gb = _norm_matmul(xp, norm_mix[0], w_in0, tm=512, tn=w_a)
    r3 = lambda a: a.reshape(B, T, w_a)
    bias = _bias_tiles(rel_bias, t=tq)
    o_a = _flash2("diff", r3(qa), r3(ka), r3(va), (bias, rel_bias, lq1, lk1, lq2, lk2, subln),
                  t=tq, lam_init=lam_init)
    o_b, p_s_b = _hgrn2(r3(qb), r3(fb), r3(ib), r3(gb), lb_param, gnorm_b[0], None,
                        layer=0, length=CHUNK, n_chunks=4)
    x1 = _proj_res([o_a.reshape(M, w_a), o_b.reshape(M, w_a)], w_out0, xp, tm=512)
    x2 = _mlp(x1, norm_mlp[0], w_up_b[0], w_down_b[0], tm=512)
    qc, kc, vc, qkd, vd, od, gts = _norm_matmul(x2, norm_mix[1], w_in1, tm=512, tn=w_a)
    p_lf_c, fcol, frow = _fox_gates(r3(gts), gate_bias, tc=512, nh=h_c)
    o_c = _flash2("fox", r3(qc), r3(kc), r3(vc), (fcol, frow), t=tq)
    o_d, p_c, p_n, p_m = _mlstm(r3(qkd), r3(vd), r3(od), r3(gts), gate_bias, gnorm_d[0], None,
                                length=CHUNK, n_chunks=4, i_lane0=i_lane0, f_lane0=f_lane0)
    x3 = _proj_res([o_c.reshape(M, w_a), o_d.reshape(M, w_a)], w_out1, x2, tm=512)
    y_prompt = _mlp(x3, norm_mlp[1], w_up_b[1], w_down_b[1], tm=512, final_g=norm_final).reshape(B, T, D)

    PADT = 8
    DEC_BB = 8
    xs = x_sample.reshape(S, D)
    sqa, ska, sva, sqb, sfb, sib, sgb = _norm_matmul(xs, norm_mix[0], w_in0, tm=S, tn=w_a)
    s3 = lambda a: a.reshape(S, 1, w_a)
    padt = lambda a: jnp.pad(a.reshape(S, 1, w_a), ((0, 0), (0, PADT - 1), (0, 0)))
    sh = lambda a: a.reshape(S, h_a, LANES)
    so_a = _decode_diff(page_table, sh(sqa), sh(ska), sh(sva),
                        cache_k_a.reshape(n_pool, page * h_a, LANES), cache_v_a.reshape(n_pool, page * h_a, LANES),
                        rel_bias, lq1, lk1, lq2, lk2, subln, lam_init=lam_init)
    so_b, s_s_b = _hgrn2(padt(sqb), padt(sfb), padt(sib), padt(sgb), lb_param, gnorm_b[0], state_s_b[0],
                         layer=0, length=PADT, n_chunks=1, bb=DEC_BB, t_valid=1)
    sx1 = _proj_res([so_a.reshape(S, w_a), so_b[:, 0]], w_out0, xs, tm=S)
    sx2 = _mlp(sx1, norm_mlp[0], w_up_b[0], w_down_b[0], tm=S)
    sqc, skc, svc, sqkd, svd, sod, sgts = _norm_matmul(sx2, norm_mix[1], w_in1, tm=S, tn=w_a)
    lf_pool_t = jnp.swapaxes(cache_logf_c[0], 1, 2)
    pool_t = lambda a: jnp.transpose(a[0], (0, 2, 3, 1)).reshape(n_pool, w_a, page)
    so_c, s_lf_c = _decode_fox(page_table, s3(sqc), s3(skc), s3(svc), sgts[:, :LANES].reshape(S, 1, LANES),
                               gate_bias, lf_pool_t, pool_t(cache_k_c), pool_t(cache_v_c))
    npair = h_d // 2
    m0 = jnp.pad(state_m_d[0].reshape(S, npair, 1, 2), ((0, 0), (0, 0), (0, 0), (0, LANES - 2)))
    so_d, s_c, s_n, s_m = _mlstm(padt(sqkd), padt(svd), padt(sod), padt(sgts), gate_bias, gnorm_d[0],
                                 (state_c_d[0].reshape(S, npair, LANES, LANES),
                                  state_n_d[0].reshape(S, npair, 1, LANES), m0),
                                 length=PADT, n_chunks=1, i_lane0=i_lane0, f_lane0=f_lane0, bb=DEC_BB,
                                 t_valid=1)
    sx3 = _proj_res([so_c.reshape(S, w_a), so_d[:, 0]], w_out1, sx2, tm=S)
    y_sample = _mlp(sx3, norm_mlp[1], w_up_b[1], w_down_b[1], tm=S, final_g=norm_final).reshape(S, 1, D)

    dk_d = state_c_d.shape[3]
    heads = lambda a, nb, tt, nh: a.reshape(1, nb, tt, nh, w_a // nh)
    unpair_c = lambda a, nb: a.reshape(1, nb, h_d, dk_d, LANES)
    unpair_n = lambda a, nb: a.reshape(1, nb, h_d, dk_d)
    unpair_m = lambda a, nb: a[..., :2].reshape(1, nb, h_d)
    return (y_prompt, y_sample,
            heads(ka, B, T, h_a), heads(va, B, T, h_a), p_s_b[None],
            heads(kc, B, T, h_c), heads(vc, B, T, h_c), p_lf_c[None],
            unpair_c(p_c, B), unpair_n(p_n, B), unpair_m(p_m, B),
            heads(ska, S, 1, h_a), heads(sva, S, 1, h_a), s_s_b[None],
            heads(skc, S, 1, h_c), heads(svc, S, 1, h_c), s_lf_c[None],
            unpair_c(s_c, S), unpair_n(s_n, S), unpair_m(s_m, S))
```

```python
import functools
import math

import jax
import jax.numpy as jnp
from jax import lax
from jax.experimental import pallas as pl
from jax.experimental.pallas import tpu as pltpu

F32 = jnp.float32
BF16 = jnp.bfloat16
EPS = 1e-6
NEG = -1e30
LOG2E = math.log2(math.e)

LANES = 128
HALF = 64
NUM_BUCKETS = 32
MAX_EXACT = NUM_BUCKETS // 2
MAX_DISTANCE = 128
CHUNK = 64
SUB = 16
VMEM_LIMIT_BYTES = 56 * 1024 * 1024


def _params(*sem):
    return pltpu.CompilerParams(dimension_semantics=sem, vmem_limit_bytes=VMEM_LIMIT_BYTES)


def _dot(a, b):
    return jnp.dot(a, b, preferred_element_type=F32)


def _dot_nt(a, b):
    return lax.dot_general(a, b, (((1,), (1,)), ((), ())), preferred_element_type=F32)


def _dot_tn(a, b):
    return lax.dot_general(a, b, (((0,), (0,)), ((), ())), preferred_element_type=F32)


def _split3(x):
    hi = x.astype(BF16)
    r = x - hi.astype(F32)
    mid = r.astype(BF16)
    lo = (r - mid.astype(F32)).astype(BF16)
    return hi, mid, lo


def _exact_dot(onesmat, x):
    hi, mid, lo = _split3(x)
    return _dot(onesmat, hi) + _dot(onesmat, mid) + _dot(onesmat, lo)


def _exact_dot_tn(x, onesmat):
    hi, mid, lo = _split3(x)
    return _dot_tn(hi, onesmat) + _dot_tn(mid, onesmat) + _dot_tn(lo, onesmat)


def _exact_dot_r(x, onesmat):
    hi, mid, lo = _split3(x)
    return _dot(hi, onesmat) + _dot(mid, onesmat) + _dot(lo, onesmat)


def _cumsum_rows(x):
    n = x.shape[0]
    row = lax.broadcasted_iota(jnp.int32, x.shape, 0)
    shift = 1
    while shift < n:
        x = x + jnp.where(row >= shift, pltpu.roll(x, shift, 0), 0.0)
        shift *= 2
    return x


def _tri(n, lower):
    r = lax.broadcasted_iota(jnp.int32, (n, n), 0)
    c = lax.broadcasted_iota(jnp.int32, (n, n), 1)
    keep = (c <= r) if lower else (r <= c)
    return jnp.where(keep, 1.0, 0.0).astype(BF16)


def _eye(n):
    r = lax.broadcasted_iota(jnp.int32, (n, n), 0)
    c = lax.broadcasted_iota(jnp.int32, (n, n), 1)
    return jnp.where(r == c, 1.0, 0.0).astype(BF16)


def _log_sigmoid(x):
    return jnp.minimum(x, 0.0) - jnp.log1p(jnp.exp(-jnp.abs(x)))


def _sigmoid(x):
    return 1.0 / (1.0 + jnp.exp(-x))


def _rms(x, g):
    return x * lax.rsqrt(jnp.mean(x * x, axis=-1, keepdims=True) + EPS) * g


def _norm_mm_kernel(x_ref, g_ref, w_ref, *o_refs, tn):
    h = _rms(x_ref[...], g_ref[...]).astype(BF16)
    for j, o_ref in enumerate(o_refs):
        o_ref[...] = _dot(h, w_ref[:, j * tn:(j + 1) * tn])


def _norm_matmul(x, g, w, *, tm, tn):
    m, d = x.shape
    n = w.shape[1]
    nj = n // tn
    return pl.pallas_call(
        functools.partial(_norm_mm_kernel, tn=tn),
        grid=(m // tm,),
        in_specs=[pl.BlockSpec((tm, d), lambda i: (i, 0)),
                  pl.BlockSpec((1, d), lambda i: (0, 0)),
                  pl.BlockSpec((d, n), lambda i: (0, 0))],
        out_specs=[pl.BlockSpec((tm, tn), lambda i: (i, 0))] * nj,
        out_shape=[jax.ShapeDtypeStruct((m, tn), F32)] * nj,
        compiler_params=_params("parallel"),
        name="norm_matmul",
    )(x, g.reshape(1, d), w)


def _mlp_kernel(x_ref, g_ref, wu_ref, wd_ref, *rest, tf, final_norm):
    o_ref = rest[-1]
    x = x_ref[...]
    h = _rms(x, g_ref[...]).astype(BF16)
    acc = x
    for c in range(wu_ref.shape[1] // tf):
        u = _dot(h, wu_ref[:, c * tf:(c + 1) * tf])
        u = jnp.square(jnp.maximum(u, 0.0)).astype(BF16)
        acc = acc + _dot(u, wd_ref[c * tf:(c + 1) * tf, :])
    if final_norm:
        acc = _rms(acc, rest[0][...])
    o_ref[...] = acc


def _mlp(x, g, w_up, w_down, *, tm, tf=512, final_g=None):
    m, d = x.shape
    f = w_up.shape[1]
    in_specs = [pl.BlockSpec((tm, d), lambda i: (i, 0)),
                pl.BlockSpec((1, d), lambda i: (0, 0)),
                pl.BlockSpec((d, f), lambda i: (0, 0)),
                pl.BlockSpec((f, d), lambda i: (0, 0))]
    args = [x, g.reshape(1, d), w_up, w_down]
    if final_g is not None:
        in_specs.append(pl.BlockSpec((1, d), lambda i: (0, 0)))
        args.append(final_g.reshape(1, d))
    return pl.pallas_call(
        functools.partial(_mlp_kernel, tf=tf, final_norm=final_g is not None),
        grid=(m // tm,),
        in_specs=in_specs,
        out_specs=pl.BlockSpec((tm, d), lambda i: (i, 0)),
        out_shape=jax.ShapeDtypeStruct((m, d), F32),
        compiler_params=_params("parallel"),
        name="mlp",
    )(*args)


def _proj_res_kernel(*refs, n_a, ks):
    a_refs, w_ref, res_ref, o_ref = refs[:n_a], refs[n_a], refs[n_a + 1], refs[n_a + 2]
    acc = res_ref[...]
    off = 0
    for a_ref, k in zip(a_refs, ks):
        acc = acc + _dot(a_ref[...], w_ref[off:off + k, :])
        off += k
    o_ref[...] = acc


def _proj_res(a_list, w, res, *, tm):
    m, n = res.shape
    ks = tuple(a.shape[1] for a in a_list)
    in_specs = [pl.BlockSpec((tm, k), lambda i: (i, 0)) for k in ks]
    in_specs += [pl.BlockSpec(w.shape, lambda i: (0, 0)),
                 pl.BlockSpec((tm, n), lambda i: (i, 0))]
    return pl.pallas_call(
        functools.partial(_proj_res_kernel, n_a=len(a_list), ks=ks),
        grid=(m // tm,),
        in_specs=in_specs,
        out_specs=pl.BlockSpec((tm, n), lambda i: (i, 0)),
        out_shape=jax.ShapeDtypeStruct((m, n), F32),
        compiler_params=_params("parallel"),
        name="proj_res",
    )(*a_list, w, res)


def _t5_bias_value(n, tab_ref, h):
    nf = jnp.maximum(n, 1).astype(F32)
    large = MAX_EXACT + (jnp.log(nf / MAX_EXACT) / math.log(MAX_DISTANCE / MAX_EXACT)
                         * (NUM_BUCKETS - MAX_EXACT)).astype(jnp.int32)
    large = jnp.minimum(large, NUM_BUCKETS - 1)
    bucket = jnp.where(n < MAX_EXACT, n, large)
    val = jnp.zeros(n.shape, F32)
    for b in range(NUM_BUCKETS):
        val = jnp.where(bucket == b, tab_ref[b, h], val)
    return val


def _bias_tiles_kernel(tab_ref, o_ref, *, t):
    h, d = pl.program_id(0), pl.program_id(1)
    i = lax.broadcasted_iota(jnp.int32, (t, t), 0)
    j = lax.broadcasted_iota(jnp.int32, (t, t), 1)
    o_ref[0, 0] = _t5_bias_value(jnp.maximum(d * t + j - i, 0), tab_ref, h) * LOG2E


def _bias_tiles(rel_bias, *, t):
    nh = rel_bias.shape[1]
    return pl.pallas_call(
        functools.partial(_bias_tiles_kernel, t=t),
        grid=(nh, 2),
        in_specs=[pl.BlockSpec(memory_space=pltpu.SMEM)],
        out_specs=pl.BlockSpec((1, 1, t, t), lambda h, d: (h, d, 0, 0)),
        out_shape=jax.ShapeDtypeStruct((nh, 2, t, t), F32),
        compiler_params=_params("arbitrary", "arbitrary"),
        name="t5_bias_tiles",
    )(rel_bias)


def _lam_value(lq1, lk1, lq2, lk2, lam_init):
    s1 = jnp.sum(lq1[...] * lk1[...], axis=-1, keepdims=True)
    s2 = jnp.sum(lq2[...] * lk2[...], axis=-1, keepdims=True)
    return jnp.exp(s1) - jnp.exp(s2) + lam_init


def _flash2_kernel(*refs, mode, t, lam_init):
    if mode == "diff":
        (q_ref, k_ref, v_ref, bias_ref, tab_ref, lq1, lk1, lq2, lk2, subln_ref,
         o_ref, m_scr, l_scr, acc_scr, kb_scr, vt_scr) = refs
    else:
        q_ref, k_ref, v_ref, fq_ref, fk_ref, o_ref, m_scr, l_scr, acc_scr, kb_scr, vt_scr = refs
    hg, qi = pl.program_id(1), pl.program_id(2)
    n_blk = k_ref.shape[1] // t

    @pl.when(qi == 0)
    def _():
        for kb in range(n_blk):
            kb_scr[kb * t:(kb + 1) * t, :] = k_ref[0, kb * t:(kb + 1) * t, :].astype(BF16)
            vt_scr[:, kb * t:(kb + 1) * t] = v_ref[0, kb * t:(kb + 1) * t, :].T.astype(BF16)

    lane = lax.broadcasted_iota(jnp.int32, (t, LANES), 1)
    q = q_ref[0] * (HALF ** -0.5 * LOG2E)
    qs = (jnp.where(lane < HALF, q, 0.0).astype(BF16),
          jnp.where(lane >= HALF, q, 0.0).astype(BF16))
    m_scr[...] = jnp.full(m_scr.shape, NEG, F32)
    l_scr[...] = jnp.zeros(l_scr.shape, F32)
    acc_scr[...] = jnp.zeros(acc_scr.shape, F32)
    if mode == "fox":
        fq = fq_ref[0]
        fq_row = lax.broadcasted_iota(jnp.int32, fq.shape, 0)
        fq_rows = [jnp.sum(jnp.where(fq_row == 2 * hg + j, fq, 0.0), axis=0, keepdims=True) * LOG2E
                   for j in range(2)]

    def step(kb, kind):
        start = pl.multiple_of(kb * t, t)
        k = kb_scr[pl.ds(start, t), :]
        vt = vt_scr[:, pl.ds(start, t)]
        for j in range(2):
            s = _dot_nt(k, qs[j])
            if mode == "diff":
                if kind == "far":
                    shift = tab_ref[NUM_BUCKETS - 1, hg] * LOG2E
                else:
                    shift = 0.0
                    s = s + bias_ref[0, 0 if kind == "diag" else 1]
            else:
                fk = fk_ref[0, pl.ds(start, t), :]
                fk_col = jnp.sum(jnp.where(lax.broadcasted_iota(jnp.int32, fk.shape, 1) == 2 * hg + j, fk, 0.0),
                                 axis=-1, keepdims=True)
                s = s - fk_col * LOG2E
                shift = fq_rows[j]
            if kind == "diag":
                r = lax.broadcasted_iota(jnp.int32, (t, t), 0)
                c = lax.broadcasted_iota(jnp.int32, (t, t), 1)
                s = jnp.where(r <= c, s, NEG)
            m_prev = m_scr[j]
            m_new = jnp.maximum(m_prev, jnp.max(s, axis=0, keepdims=True) + shift)
            alpha = jnp.exp2(m_prev - m_new)
            p = jnp.exp2(s - (m_new - shift))
            l_scr[j] = alpha * l_scr[j] + jnp.sum(p, axis=0, keepdims=True)
            acc_scr[j] = alpha * acc_scr[j] + _dot(vt, p.astype(BF16))
            m_scr[j] = m_new

    if mode == "diff":
        def far_body(kb, carry):
            step(kb, "far")
            return carry
        lax.fori_loop(0, qi - 1, far_body, 0)

        @pl.when(qi >= 1)
        def _():
            step(qi - 1, "sub")
    else:
        def far_body(kb, carry):
            step(kb, "far")
            return carry
        lax.fori_loop(0, qi, far_body, 0)
    step(qi, "diag")

    o0 = acc_scr[0] / l_scr[0]
    o1 = acc_scr[1] / l_scr[1]
    if mode == "diff":
        lam = _lam_value(lq1, lk1, lq2, lk2, lam_init)
        o = (o0 - lam * o1).T
        o_ref[0] = (_rms(o, subln_ref[...]) * (1.0 - lam_init)).astype(o_ref.dtype)
    else:
        row = lax.broadcasted_iota(jnp.int32, (LANES, t), 0)
        o_ref[0] = jnp.where(row < HALF, o0, o1).T.astype(o_ref.dtype)


def _flash2(mode, q, k, v, extra, *, t, lam_init=0.0):
    b, tt, w = q.shape
    ng = w // LANES
    qspec = pl.BlockSpec((1, t, LANES), lambda bi, g, qi: (bi, qi, g))
    kvspec = pl.BlockSpec((1, tt, LANES), lambda bi, g, qi: (bi, 0, g))
    small = lambda shape: pl.BlockSpec(shape, lambda bi, g, qi: (0,) * len(shape))
    if mode == "diff":
        bias, tab, lq1, lk1, lq2, lk2, subln = extra
        in_specs = [qspec, kvspec, kvspec,
                    pl.BlockSpec((1, 2, t, t), lambda bi, g, qi: (g, 0, 0, 0)),
                    pl.BlockSpec(memory_space=pltpu.SMEM),
                    small((1, HALF)), small((1, HALF)), small((1, HALF)), small((1, HALF)),
                    small((1, LANES))]
        args = (q, k, v, bias, tab, lq1, lk1, lq2, lk2, subln)
    else:
        fcol, frow = extra
        nh = fcol.shape[-1]
        in_specs = [qspec, kvspec, kvspec,
                    pl.BlockSpec((1, nh, t), lambda bi, g, qi: (bi, 0, qi)),
                    pl.BlockSpec((1, tt, nh), lambda bi, g, qi: (bi, 0, 0))]
        args = (q, k, v, frow, fcol)
    return pl.pallas_call(
        functools.partial(_flash2_kernel, mode=mode, t=t, lam_init=lam_init),
        grid=(b, ng, tt // t),
        in_specs=in_specs,
        out_specs=qspec,
        out_shape=jax.ShapeDtypeStruct((b, tt, w), BF16),
        scratch_shapes=[pltpu.VMEM((2, 1, t), F32), pltpu.VMEM((2, 1, t), F32),
                        pltpu.VMEM((2, LANES, t), F32),
                        pltpu.VMEM((tt, LANES), BF16), pltpu.VMEM((LANES, tt), BF16)],
        compiler_params=_params("parallel", "parallel", "arbitrary"),
        name="flash2_" + mode,
    )(*args)


def _fox_gates_kernel(g_ref, bias_ref, lf_ref, fcol_ref, frow_ref, crow_scr, ccol_scr, *, tc, nh):
    @pl.when(pl.program_id(1) == 0)
    def _():
        crow_scr[...] = jnp.zeros(crow_scr.shape, F32)
        ccol_scr[...] = jnp.zeros(ccol_scr.shape, F32)

    lf = _log_sigmoid(g_ref[0] + bias_ref[...])
    lf_ref[0] = lf[:, :nh]
    fcol = _exact_dot(_tri(tc, True), lf) + crow_scr[...]
    frow = _exact_dot_tn(lf, _tri(tc, False)) + ccol_scr[...]
    fcol_ref[0] = fcol[:, :nh]
    frow_ref[0] = frow[:nh, :]
    crow_scr[...] = fcol[tc - 1:tc, :]
    ccol_scr[...] = frow[:, tc - 1:tc]


def _fox_gates(gates, bias_row, *, tc, nh):
    b, tt, _ = gates.shape
    return pl.pallas_call(
        functools.partial(_fox_gates_kernel, tc=tc, nh=nh),
        grid=(b, tt // tc),
        in_specs=[pl.BlockSpec((1, tc, LANES), lambda bi, c: (bi, c, 0)),
                  pl.BlockSpec((1, LANES), lambda bi, c: (0, 0))],
        out_specs=[pl.BlockSpec((1, tc, nh), lambda bi, c: (bi, c, 0)),
                   pl.BlockSpec((1, tc, nh), lambda bi, c: (bi, c, 0)),
                   pl.BlockSpec((1, nh, tc), lambda bi, c: (bi, 0, c))],
        out_shape=[jax.ShapeDtypeStruct((b, tt, nh), F32),
                   jax.ShapeDtypeStruct((b, tt, nh), F32),
                   jax.ShapeDtypeStruct((b, nh, tt), F32)],
        scratch_shapes=[pltpu.VMEM((1, LANES), F32), pltpu.VMEM((LANES, 1), F32)],
        compiler_params=_params("parallel", "arbitrary"),
        name="fox_gates",
    )(gates, bias_row)


def _hgrn2_kernel(*refs, layer, n_layers, length, n_chunks, bb, t_valid, has_state):
    if has_state:
        q_ref, f_ref, i_ref, g_ref, lbp_ref, gn_ref, s0_ref, o_ref, s_ref, st_scr = refs
    else:
        q_ref, f_ref, i_ref, g_ref, lbp_ref, gn_ref, o_ref, s_ref, st_scr = refs
    tb = pl.program_id(2)

    @pl.when(tb == 0)
    def _():
        for bi in range(bb):
            if has_state:
                st_scr[bi] = s0_ref[bi, 0].T
            else:
                st_scr[bi] = jnp.zeros(st_scr.shape[1:], F32)

    lbp = lbp_ref[...]
    e = jnp.exp(lbp - jnp.max(lbp, axis=0, keepdims=True))
    sm = e / jnp.sum(e, axis=0, keepdims=True)
    lb = jnp.sum(sm[:layer + 1], axis=0, keepdims=True)

    L = length
    C = min(SUB, L)
    tril = _tri(L, True)
    row_c = lax.broadcasted_iota(jnp.int32, (C, LANES), 0)

    states = [st_scr[bi] for bi in range(bb)]
    for c, bi in [(c, bi) for c in range(n_chunks) for bi in range(bb)]:
        rows = slice(c * L, (c + 1) * L)
        f = lb + (1.0 - lb) * _sigmoid(f_ref[bi, rows, :])
        logf = jnp.log(f)
        kk = 1.0 - f
        if t_valid is not None:
            valid = lax.broadcasted_iota(jnp.int32, (L, LANES), 0) < t_valid
            logf = jnp.where(valid, logf, 0.0)
            kk = jnp.where(valid, kk, 0.0)
        q = q_ref[bi, rows, :] * (LANES ** -0.5)
        iv = i_ref[bi, rows, :]
        bcum = _cumsum_rows(logf)
        st = states[bi]
        o_inter = _dot_nt((q * jnp.exp(bcum)).astype(BF16), st.astype(BF16))
        for sc in range(L // C):
            lo = sc * C
            q_s, b_s = q[lo:lo + C], bcum[lo:lo + C]
            o_s = o_inter[lo:lo + C]
            if sc > 0:
                ref_row = bcum[lo - 1:lo]
                qt = q_s * jnp.exp(b_s - ref_row)
                kt = kk[:lo] * jnp.exp(ref_row - bcum[:lo])
                a = _dot_nt(qt.astype(BF16), kt.astype(BF16))
                o_s = o_s + _dot(a.astype(BF16), iv[:lo].astype(BF16))
            b2_s = b_s * LOG2E
            for s in range(C):
                dec = jnp.where(row_c >= s, jnp.exp2(b2_s - b2_s[s:s + 1]), 0.0)
                a_col = jnp.sum(q_s * kk[lo + s:lo + s + 1] * dec, axis=-1, keepdims=True)
                o_s = o_s + a_col * iv[lo + s:lo + s + 1]
            gate = g_ref[bi, c * L + lo:c * L + lo + C, :]
            o_ref[bi, c * L + lo:c * L + lo + C, :] = (
                _rms(o_s, gn_ref[...]) * (gate * _sigmoid(gate))).astype(o_ref.dtype)
        b_last = bcum[L - 1:L]
        kd = kk * jnp.exp(b_last - bcum)
        states[bi] = st * jnp.exp(b_last) + _dot_tn(iv.astype(BF16), kd.astype(BF16))
    for bi in range(bb):
        st_scr[bi] = states[bi]

    @pl.when(tb == pl.num_programs(2) - 1)
    def _():
        for bi in range(bb):
            s_ref[bi, 0] = st_scr[bi].T


def _hgrn2(q, f, i, g, lb_param, gnorm, s0, *, layer, length, n_chunks, bb=1, t_valid=None):
    b, tt, w = q.shape
    nh = w // LANES
    tb = length * n_chunks
    assert b % bb == 0 and tt % tb == 0, (b, bb, tt, tb)
    xspec = pl.BlockSpec((bb, tb, LANES), lambda bi, h, ti: (bi, ti, h))
    sspec = pl.BlockSpec((bb, 1, LANES, LANES), lambda bi, h, ti: (bi, h, 0, 0))
    nl = lb_param.shape[0]
    in_specs = [xspec, xspec, xspec, xspec,
                pl.BlockSpec((nl, LANES), lambda bi, h, ti: (0, h)),
                pl.BlockSpec((1, LANES), lambda bi, h, ti: (0, 0))]
    args = [q, f, i, g, lb_param, gnorm.reshape(1, LANES)]
    if s0 is not None:
        in_specs.append(sspec)
        args.append(s0)
    return pl.pallas_call(
        functools.partial(_hgrn2_kernel, layer=layer, n_layers=nl - 1, length=length,
                          n_chunks=n_chunks, bb=bb, t_valid=t_valid, has_state=s0 is not None),
        grid=(b // bb, nh, tt // tb),
        in_specs=in_specs,
        out_specs=[xspec, sspec],
        out_shape=[jax.ShapeDtypeStruct((b, tt, w), BF16),
                   jax.ShapeDtypeStruct((b, nh, LANES, LANES), F32)],
        scratch_shapes=[pltpu.VMEM((bb, LANES, LANES), F32)],
        compiler_params=_params("parallel", "parallel", "arbitrary"),
        name="hgrn2",
    )(*args)


def _mlstm_kernel(*refs, length, n_chunks, bb, t_valid, has_state, i_lane0, f_lane0):
    if has_state:
        (q_ref, k_ref, v_ref, og_ref, gt_ref, gb_ref, gn_ref, c0_ref, n0_ref, m0_ref,
         o_ref, c_ref, n_ref, m_ref, c_scr, n_scr, m_scr) = refs
    else:
        (q_ref, k_ref, v_ref, og_ref, gt_ref, gb_ref, gn_ref,
         o_ref, c_ref, n_ref, m_ref, c_scr, n_scr, m_scr) = refs
    hp, tb = pl.program_id(1), pl.program_id(2)

    @pl.when(tb == 0)
    def _():
        top = lax.broadcasted_iota(jnp.int32, (LANES, LANES), 0) < HALF
        for bi in range(bb):
            if has_state:
                c0 = c0_ref[bi, 0]
                c_scr[bi] = jnp.concatenate([jnp.where(top, c0, 0.0), jnp.where(top, 0.0, c0)], axis=1)
                n_scr[bi] = n0_ref[bi, 0]
                m_scr[bi] = m0_ref[bi, 0]
            else:
                c_scr[bi] = jnp.zeros(c_scr.shape[1:], F32)
                n_scr[bi] = jnp.zeros(n_scr.shape[1:], F32)
                m_scr[bi] = jnp.zeros(m_scr.shape[1:], F32)

    L = length
    lane = lax.broadcasted_iota(jnp.int32, (L, LANES), 1)
    lane2 = lax.broadcasted_iota(jnp.int32, (L, 2 * L), 1)
    row2 = lax.broadcasted_iota(jnp.int32, (L, 2 * L), 0)
    left2 = lane2 < L
    causal2 = jnp.where(left2, lane2, lane2 - L) <= row2
    left_v = lax.broadcasted_iota(jnp.int32, (L, 2 * LANES), 1) < LANES
    left_q = lane < HALF
    nlane = lax.broadcasted_iota(jnp.int32, (1, LANES), 1)
    crow = lax.broadcasted_iota(jnp.int32, (LANES, 2 * LANES), 0)
    ccol = lax.broadcasted_iota(jnp.int32, (LANES, 2 * LANES), 1)
    own_block = (crow < HALF) == (ccol < LANES)
    is_f = (lane >= f_lane0) & (lane < f_lane0 + 4)
    two = lambda cond, a: jnp.where(cond, a[0], a[1])
    cs = [c_scr[bi] for bi in range(bb)]
    ns = [n_scr[bi] for bi in range(bb)]
    ms = [m_scr[bi] for bi in range(bb)]
    for c, bi in [(c, bi) for c in range(n_chunks) for bi in range(bb)]:
        rows = slice(c * L, (c + 1) * L)
        gpre = gt_ref[bi, rows, :] + gb_ref[...]
        x = jnp.where(is_f, _log_sigmoid(gpre), gpre)
        if t_valid is not None:
            valid = lax.broadcasted_iota(jnp.int32, (L, LANES), 0) < t_valid
            x = jnp.where(valid, x, jnp.where(is_f, 0.0, NEG))
        bcol_all = _cumsum_rows(x)
        xt = x.T
        brow_all = bcol_all.T
        srow = lax.broadcasted_iota(jnp.int32, xt.shape, 0)
        pick = lambda a, l: jnp.sum(jnp.where(lane == l, a, 0.0), axis=-1, keepdims=True)
        pick_t = lambda a, l: jnp.sum(jnp.where(srow == l, a, 0.0), axis=0, keepdims=True)
        heads = (2 * hp, 2 * hp + 1)
        b_col = [pick(bcol_all, f_lane0 + h) for h in heads]
        ig_col = [pick(x, i_lane0 + h) for h in heads]
        b_row2 = jnp.concatenate([pick_t(brow_all, f_lane0 + h) for h in heads], axis=1)
        ig_row2 = jnp.concatenate([pick_t(xt, i_lane0 + h) for h in heads], axis=1)
        m_old = [jnp.sum(jnp.where(nlane == j, ms[bi], 0.0), axis=-1, keepdims=True) for j in range(2)]
        d2 = jnp.where(causal2, two(left2, b_col) - b_row2 + ig_row2, NEG)
        inter = [b_col[j] + m_old[j] for j in range(2)]
        m_t = [jnp.maximum(inter[0], jnp.max(jnp.where(left2, d2, NEG), axis=-1, keepdims=True)),
               jnp.maximum(inter[1], jnp.max(jnp.where(left2, NEG, d2), axis=-1, keepdims=True))]
        w_inter = [jnp.exp(inter[j] - m_t[j]) for j in range(2)]
        wmat2 = jnp.exp(d2 - two(left2, m_t))

        q = q_ref[bi, rows, :]
        k = k_ref[bi, rows, :] * (HALF ** -0.5)
        v = v_ref[bi, rows, :]
        qb = q.astype(BF16)
        cst, nst = cs[bi], ns[bi]
        k_bd = jnp.concatenate([jnp.where(left_q, k, 0.0), jnp.where(left_q, 0.0, k)], axis=0)
        v_bd = jnp.concatenate([jnp.where(left_v, v, 0.0), jnp.where(left_v, 0.0, v)], axis=0)
        qk2 = _dot_nt(qb, k_bd.astype(BF16)) * wmat2
        num2 = (two(left_v, w_inter) * _dot(qb, cst.astype(BF16))
                + _dot(qk2.astype(BF16), v_bd.astype(BF16)))
        qn = q * nst
        den = [w_inter[0] * jnp.sum(jnp.where(left_q, qn, 0.0), axis=-1, keepdims=True)
               + jnp.sum(jnp.where(left2, qk2, 0.0), axis=-1, keepdims=True),
               w_inter[1] * jnp.sum(jnp.where(left_q, 0.0, qn), axis=-1, keepdims=True)
               + jnp.sum(jnp.where(left2, 0.0, qk2), axis=-1, keepdims=True)]
        floor = [jnp.maximum(jnp.abs(den[j]), jnp.exp(-m_t[j])) for j in range(2)]
        hd2 = num2 / two(left_v, floor)
        og = og_ref[bi, rows, :]
        for j in range(2):
            cols = slice(j * LANES, (j + 1) * LANES)
            o_ref[bi, rows, cols] = (_rms(hd2[:, cols], gn_ref[...]) * _sigmoid(og[:, cols])).astype(o_ref.dtype)

        m_new = [m_t[j][L - 1:L] for j in range(2)]
        b_last = [b_col[j][L - 1:L] for j in range(2)]
        w_c = [jnp.exp(b_last[j] + m_old[j] - m_new[j]) for j in range(2)]
        w_s = [jnp.exp(b_last[j] - b_col[j] + ig_col[j] - m_new[j]) for j in range(2)]
        ks = k * two(left_q, w_s)
        upd = _dot_tn(ks.astype(BF16), v.astype(BF16))
        cs[bi] = two(crow < HALF, w_c) * cst + jnp.where(own_block, upd, 0.0)
        ns[bi] = two(nlane < HALF, w_c) * nst + jnp.sum(ks, axis=0, keepdims=True)
        ms[bi] = jnp.where(nlane == 0, m_new[0], jnp.where(nlane == 1, m_new[1], ms[bi]))
    for bi in range(bb):
        c_scr[bi], n_scr[bi], m_scr[bi] = cs[bi], ns[bi], ms[bi]

    @pl.when(tb == pl.num_programs(2) - 1)
    def _():
        for bi in range(bb):
            c_ref[bi, 0] = c_scr[bi, :, :LANES] + c_scr[bi, :, LANES:]
            n_ref[bi, 0] = n_scr[bi]
            m_ref[bi, 0] = m_scr[bi]


def _mlstm(qk, v, og, gates, gate_bias, gnorm, state, *, length, n_chunks, i_lane0, f_lane0, bb=1,
           t_valid=None):
    b, tt, _ = qk.shape
    tb = length * n_chunks
    assert b % bb == 0 and tt % tb == 0, (b, bb, tt, tb)
    npair = 2
    qspec = pl.BlockSpec((bb, tb, LANES), lambda bi, hp, ti: (bi, ti, hp))
    kspec = pl.BlockSpec((bb, tb, LANES), lambda bi, hp, ti: (bi, ti, npair + hp))
    vspec = pl.BlockSpec((bb, tb, 2 * LANES), lambda bi, hp, ti: (bi, ti, hp))
    gspec = pl.BlockSpec((bb, tb, LANES), lambda bi, hp, ti: (bi, ti, 0))
    row = pl.BlockSpec((1, LANES), lambda bi, hp, ti: (0, 0))
    cspec = pl.BlockSpec((bb, 1, LANES, LANES), lambda bi, hp, ti: (bi, hp, 0, 0))
    nspec = pl.BlockSpec((bb, 1, 1, LANES), lambda bi, hp, ti: (bi, hp, 0, 0))
    in_specs = [qspec, kspec, vspec, vspec, gspec, row, row]
    args = [qk, qk, v, og, gates, gate_bias, gnorm.reshape(1, LANES)]
    if state is not None:
        in_specs += [cspec, nspec, nspec]
        args += list(state)
    return pl.pallas_call(
        functools.partial(_mlstm_kernel, length=length, n_chunks=n_chunks, bb=bb, t_valid=t_valid,
                          has_state=state is not None, i_lane0=i_lane0, f_lane0=f_lane0),
        grid=(b // bb, npair, tt // tb),
        in_specs=in_specs,
        out_specs=[vspec, cspec, nspec, nspec],
        out_shape=[jax.ShapeDtypeStruct((b, tt, 4 * LANES), BF16),
                   jax.ShapeDtypeStruct((b, npair, LANES, LANES), F32),
                   jax.ShapeDtypeStruct((b, npair, 1, LANES), F32),
                   jax.ShapeDtypeStruct((b, npair, 1, LANES), F32)],
        scratch_shapes=[pltpu.VMEM((bb, LANES, 2 * LANES), F32), pltpu.VMEM((bb, 1, LANES), F32),
                        pltpu.VMEM((bb, 1, LANES), F32)],
        compiler_params=_params("parallel", "parallel", "arbitrary"),
        name="mlstm",
    )(*args)


ROWS = 16


def _pad_rows(x, rows):
    return jnp.concatenate([x, jnp.zeros((rows - x.shape[0], x.shape[1]), x.dtype)], axis=0)


def _decode_diff_kernel(pt_ref, q_ref, kn_ref, vn_ref, tab_ref, lq1, lk1, lq2, lk2, subln_ref, *rest,
                        n_pages, page, n_heads, lam_init):
    del pt_ref
    k_refs, v_refs = rest[:n_pages], rest[n_pages:2 * n_pages]
    o_ref, bias_scr = rest[2 * n_pages], rest[2 * n_pages + 1]
    past = n_pages * page
    pw = page * n_heads
    rr = lax.broadcasted_iota(jnp.int32, (ROWS, pw), 0) % 8
    ll = lax.broadcasted_iota(jnp.int32, (ROWS, pw), 1)
    own = (ll % n_heads == rr) & (rr < n_heads)
    rcol = lax.broadcasted_iota(jnp.int32, (ROWS, 1), 0) % 8

    @pl.when(pl.program_id(0) == 0)
    def _():
        for p in range(n_pages):
            n = past - (p * page + ll // n_heads)
            acc = jnp.zeros((ROWS, pw), F32)
            for h in range(n_heads):
                acc = jnp.where(rr == h, _t5_bias_value(n, tab_ref, h), acc)
            bias_scr[:, p * pw:(p + 1) * pw] = acc

    lane = lax.broadcasted_iota(jnp.int32, (8, LANES), 1)
    q8 = _pad_rows(q_ref[0], 8) * (HALF ** -0.5)
    qm = jnp.concatenate([jnp.where(lane < HALF, q8, 0.0), jnp.where(lane >= HALF, q8, 0.0)], axis=0)
    qb = qm.astype(BF16)
    kn8 = _pad_rows(kn_ref[0], 8)
    bias_new = jnp.zeros((ROWS, 1), F32)
    for h in range(n_heads):
        bias_new = jnp.where(rcol == h, tab_ref[0, h], bias_new)
    s_new = jnp.sum(qm * jnp.concatenate([kn8, kn8], axis=0), axis=-1, keepdims=True) + bias_new
    s_pages = [jnp.where(own, _dot_nt(qb, k_refs[p][...].astype(BF16)) + bias_scr[:, p * pw:(p + 1) * pw], NEG)
               for p in range(n_pages)]
    m = s_new
    for s in s_pages:
        m = jnp.maximum(m, jnp.max(s, axis=-1, keepdims=True))
    e_new = jnp.exp(s_new - m)
    e_pages = [jnp.exp(s - m) for s in s_pages]
    l = e_new
    for e in e_pages:
        l = l + jnp.sum(e, axis=-1, keepdims=True)
    lam = _lam_value(lq1, lk1, lq2, lk2, lam_init)
    inv = 1.0 / l
    comb = lambda a: a[0:8] - lam * a[8:16]
    out = comb(e_new * inv) * _pad_rows(vn_ref[0], 8)
    for p in range(n_pages):
        wgt = _pad_rows(comb(e_pages[p] * inv), ROWS).astype(BF16)
        out = out + _dot(wgt, v_refs[p][...].astype(BF16))[0:8]
    y = _rms(out, subln_ref[...]) * (1.0 - lam_init)
    o_ref[0] = y[0:n_heads].astype(o_ref.dtype)


def _decode_diff(page_table, q, k_new, v_new, k_pool, v_pool, rel_bias, lq1, lk1, lq2, lk2, subln, *, lam_init):
    ns, n_pages = page_table.shape
    n_heads = q.shape[1]
    pw = k_pool.shape[1]
    row = pl.BlockSpec((1, n_heads, LANES), lambda b, pt: (b, 0, 0))
    small = lambda shape: pl.BlockSpec(shape, lambda b, pt: (0,) * len(shape))
    page_spec = lambda p: pl.BlockSpec((None, pw, LANES), lambda b, pt: (pt[b, p], 0, 0))
    in_specs = [row, row, row, pl.BlockSpec(memory_space=pltpu.SMEM),
                small((1, HALF)), small((1, HALF)), small((1, HALF)), small((1, HALF)), small((1, LANES))]
    in_specs += [page_spec(p) for p in range(n_pages)] * 2
    grid_spec = pltpu.PrefetchScalarGridSpec(
        num_scalar_prefetch=1, grid=(ns,), in_specs=in_specs, out_specs=row,
        scratch_shapes=[pltpu.VMEM((ROWS, n_pages * pw), F32)])
    return pl.pallas_call(
        functools.partial(_decode_diff_kernel, n_pages=n_pages, page=pw // n_heads, n_heads=n_heads,
                          lam_init=lam_init),
        grid_spec=grid_spec,
        out_shape=jax.ShapeDtypeStruct((ns, n_heads, LANES), BF16),
        compiler_params=_params("arbitrary"),
        name="decode_diff",
    )(page_table, q, k_new, v_new, rel_bias, lq1, lk1, lq2, lk2, subln,
      *([k_pool] * n_pages), *([v_pool] * n_pages))


def _decode_fox_kernel(pt_ref, q_ref, kn_ref, vn_ref, g_ref, gb_ref, *rest, n_pages, page, n_heads):
    del pt_ref
    f_refs, k_refs, v_refs = rest[:n_pages], rest[n_pages:2 * n_pages], rest[2 * n_pages:3 * n_pages]
    o_ref, lf_ref = rest[3 * n_pages], rest[3 * n_pages + 1]
    w = n_heads * HALF
    r_i = lax.broadcasted_iota(jnp.int32, (ROWS, w), 0)
    lane = lax.broadcasted_iota(jnp.int32, (ROWS, w), 1)
    q = q_ref[0] * (HALF ** -0.5)
    qbig = jnp.where(lane // HALF == r_i, q, 0.0)
    qb = qbig.astype(BF16)
    lf_row = _log_sigmoid(g_ref[0] + gb_ref[...])
    lf_ref[0] = lf_row[:, :n_heads]
    gr = lax.broadcasted_iota(jnp.int32, (ROWS, LANES), 0)
    gl = lax.broadcasted_iota(jnp.int32, (ROWS, LANES), 1)
    lf_col = jnp.sum(jnp.where((gl == gr) & (gr < n_heads), lf_row, 0.0), axis=-1, keepdims=True)
    triu = _tri(page, False)
    carry = jnp.zeros((ROWS, 1), F32)
    prefix = []
    for p in range(n_pages):
        pp = _exact_dot_r(_pad_rows(f_refs[p][...], ROWS), triu) + carry
        prefix.append(pp)
        carry = pp[:, page - 1:page]
    s_new = jnp.sum(qbig * kn_ref[0], axis=-1, keepdims=True)
    s_pages = [_dot(qb, k_refs[p][...].astype(BF16)) + (lf_col + carry - prefix[p])
               for p in range(n_pages)]
    m = s_new
    for s in s_pages:
        m = jnp.maximum(m, jnp.max(s, axis=-1, keepdims=True))
    e_new = jnp.exp(s_new - m)
    e_pages = [jnp.exp(s - m) for s in s_pages]
    l = e_new
    for e in e_pages:
        l = l + jnp.sum(e, axis=-1, keepdims=True)
    inv = 1.0 / l
    out = (e_new * inv) * vn_ref[0]
    for p in range(n_pages):
        out = out + _dot_nt((e_pages[p] * inv).astype(BF16), v_refs[p][...].astype(BF16))
    o_ref[0] = jnp.sum(jnp.where(lane // HALF == r_i, out, 0.0), axis=0, keepdims=True).astype(o_ref.dtype)


def _decode_fox(page_table, q, k_new, v_new, gates, gate_bias, lf_pool_t, k_pool_t, v_pool_t):
    ns, n_pages = page_table.shape
    _, w, page = k_pool_t.shape
    n_heads = w // HALF
    row = pl.BlockSpec((1, 1, w), lambda b, pt: (b, 0, 0))
    grow = pl.BlockSpec((1, 1, LANES), lambda b, pt: (b, 0, 0))
    page_spec = lambda p: pl.BlockSpec((None, w, page), lambda b, pt: (pt[b, p], 0, 0))
    f_spec = lambda p: pl.BlockSpec((None, n_heads, page), lambda b, pt: (pt[b, p], 0, 0))
    in_specs = [row, row, row, grow, pl.BlockSpec((1, LANES), lambda b, pt: (0, 0))]
    in_specs += [f_spec(p) for p in range(n_pages)] + [page_spec(p) for p in range(n_pages)] * 2
    grid_spec = pltpu.PrefetchScalarGridSpec(
        num_scalar_prefetch=1, grid=(ns,), in_specs=in_specs,
        out_specs=[row, pl.BlockSpec((1, 1, n_heads), lambda b, pt: (b, 0, 0))])
    return pl.pallas_call(
        functools.partial(_decode_fox_kernel, n_pages=n_pages, page=page, n_heads=n_heads),
        grid_spec=grid_spec,
        out_shape=[jax.ShapeDtypeStruct((ns, 1, w), BF16), jax.ShapeDtypeStruct((ns, 1, n_heads), F32)],
        compiler_params=_params("arbitrary"),
        name="decode_fox",
    )(page_table, q, k_new, v_new, gates, gate_bias,
      *([lf_pool_t] * n_pages), *([k_pool_t] * n_pages), *([v_pool_t] * n_pages))


def kernel(x_prompt, x_sample, cache_k_a, cache_v_a, state_s_b, cache_k_c, cache_v_c, cache_logf_c,
           state_c_d, state_n_d, state_m_d, page_table, norm_mix, w_in_even, lambda_q1, lambda_k1,
           lambda_q2, lambda_k2, subln_a, rel_bias, lb_param, gnorm_b, w_out_even, w_in_odd, b_f_c,
           b_i_d, b_f_d, gnorm_d, w_out_odd, norm_mlp, w_up, w_down, norm_final):
    B, T, D = x_prompt.shape
    S = x_sample.shape[0]
    n_pool, page = cache_k_a.shape[1], cache_k_a.shape[2]
    h_a, h_c, h_d = cache_k_a.shape[3], cache_k_c.shape[3], state_c_d.shape[2]
    w_a = h_a * LANES
    lam_init = 0.8 - 0.6 * math.exp(-0.3 * 0)
    tq = 512

    w_in0 = w_in_even[0].astype(BF16)
    wo = w_in_odd[0]
    c = [0]
    for sz in (h_c * HALF, h_c * HALF, h_c * HALF, h_c, h_d * HALF, h_d * HALF, h_d * LANES, h_d, h_d,
               h_d * LANES):
        c.append(c[-1] + sz)
    col = lambda i: wo[:, c[i]:c[i + 1]]
    gate_cols = jnp.concatenate([col(3), col(7), col(8)], axis=1)
    gate_cols = jnp.pad(gate_cols, ((0, 0), (0, w_a - gate_cols.shape[1])))
    w_in1 = jnp.concatenate([col(0), col(1), col(2), col(4), col(5), col(6), col(9), gate_cols],
                            axis=1).astype(BF16)
    gate_bias = jnp.pad(jnp.concatenate([b_f_c[0], b_i_d[0], b_f_d[0]]), (0, LANES - h_c - 2 * h_d))
    gate_bias = gate_bias.reshape(1, LANES)
    i_lane0, f_lane0 = h_c, h_c + h_d
    w_out0, w_out1 = w_out_even[0].astype(BF16), w_out_odd[0].astype(BF16)
    w_up_b, w_down_b = w_up.astype(BF16), w_down.astype(BF16)
    lq1, lk1, lq2, lk2 = (a.reshape(1, HALF) for a in (lambda_q1[0], lambda_k1[0], lambda_q2[0], lambda_k2[0]))
    subln = subln_a[0].reshape(1, LANES)

    M = B * T
    xp = x_prompt.reshape(M, D)
    qa, ka, va, qb, fb, ib, gb = _norm_matmul(xp, norm_mix[0], w_in0, tm=512, tn=w_a)
    r3 = lambda a: a.reshape(B, T, w_a)
    bias = _bias_tiles(rel_bias, t=tq)
    o_a = _flash2("diff", r3(qa), r3(ka), r3(va), (bias, rel_bias, lq1, lk1, lq2, lk2, subln),
                  t=tq, lam_init=lam_init)
    o_b, p_s_b = _hgrn2(r3(qb), r3(fb), r3(ib), r3(gb), lb_param, gnorm_b[0], None,
                        layer=0, length=CHUNK, n_chunks=8, bb=2)
    x1 = _proj_res([o_a.reshape(M, w_a), o_b.reshape(M, w_a)], w_out0, xp, tm=512)
    x2 = _mlp(x1, norm_mlp[0], w_up_b[0], w_down_b[0], tm=512)
    qc, kc, vc, qkd, vd, od, gts = _norm_matmul(x2, norm_mix[1], w_in1, tm=512, tn=w_a)
    p_lf_c, fcol, frow = _fox_gates(r3(gts), gate_bias, tc=512, nh=h_c)
    o_c = _flash2("fox", r3(qc), r3(kc), r3(vc), (fcol, frow), t=tq)
    o_d, p_c, p_n, p_m = _mlstm(r3(qkd), r3(vd), r3(od), r3(gts), gate_bias, gnorm_d[0], None,
                                length=CHUNK, n_chunks=4, bb=2, i_lane0=i_lane0, f_lane0=f_lane0)
    x3 = _proj_res([o_c.reshape(M, w_a), o_d.reshape(M, w_a)], w_out1, x2, tm=512)
    y_prompt = _mlp(x3, norm_mlp[1], w_up_b[1], w_down_b[1], tm=512, final_g=norm_final).reshape(B, T, D)

    PADT = 8
    DEC_BB = 8
    xs = x_sample.reshape(S, D)
    sqa, ska, sva, sqb, sfb, sib, sgb = _norm_matmul(xs, norm_mix[0], w_in0, tm=S, tn=w_a)
    s3 = lambda a: a.reshape(S, 1, w_a)
    padt = lambda a: jnp.pad(a.reshape(S, 1, w_a), ((0, 0), (0, PADT - 1), (0, 0)))
    sh = lambda a: a.reshape(S, h_a, LANES)
    so_a = _decode_diff(page_table, sh(sqa), sh(ska), sh(sva),
                        cache_k_a.reshape(n_pool, page * h_a, LANES), cache_v_a.reshape(n_pool, page * h_a, LANES),
                        rel_bias, lq1, lk1, lq2, lk2, subln, lam_init=lam_init)
    so_b, s_s_b = _hgrn2(padt(sqb), padt(sfb), padt(sib), padt(sgb), lb_param, gnorm_b[0], state_s_b[0],
                         layer=0, length=PADT, n_chunks=1, bb=DEC_BB, t_valid=1)
    sx1 = _proj_res([so_a.reshape(S, w_a), so_b[:, 0]], w_out0, xs, tm=S)
    sx2 = _mlp(sx1, norm_mlp[0], w_up_b[0], w_down_b[0], tm=S)
    sqc, skc, svc, sqkd, svd, sod, sgts = _norm_matmul(sx2, norm_mix[1], w_in1, tm=S, tn=w_a)
    lf_pool_t = jnp.swapaxes(cache_logf_c[0], 1, 2)
    pool_t = lambda a: jnp.transpose(a[0], (0, 2, 3, 1)).reshape(n_pool, w_a, page)
    so_c, s_lf_c = _decode_fox(page_table, s3(sqc), s3(skc), s3(svc), sgts[:, :LANES].reshape(S, 1, LANES),
                               gate_bias, lf_pool_t, pool_t(cache_k_c), pool_t(cache_v_c))
    npair = h_d // 2
    m0 = jnp.pad(state_m_d[0].reshape(S, npair, 1, 2), ((0, 0), (0, 0), (0, 0), (0, LANES - 2)))
    so_d, s_c, s_n, s_m = _mlstm(padt(sqkd), padt(svd), padt(sod), padt(sgts), gate_bias, gnorm_d[0],
                                 (state_c_d[0].reshape(S, npair, LANES, LANES),
                                  state_n_d[0].reshape(S, npair, 1, LANES), m0),
                                 length=PADT, n_chunks=1, i_lane0=i_lane0, f_lane0=f_lane0, bb=DEC_BB,
                                 t_valid=1)
    sx3 = _proj_res([so_c.reshape(S, w_a), so_d[:, 0]], w_out1, sx2, tm=S)
    y_sample = _mlp(sx3, norm_mlp[1], w_up_b[1], w_down_b[1], tm=S, final_g=norm_final).reshape(S, 1, D)

    dk_d = state_c_d.shape[3]
    heads = lambda a, nb, tt, nh: a.reshape(1, nb, tt, nh, w_a // nh)
    unpair_c = lambda a, nb: a.reshape(1, nb, h_d, dk_d, LANES)
    unpair_n = lambda a, nb: a.reshape(1, nb, h_d, dk_d)
    unpair_m = lambda a, nb: a[..., :2].reshape(1, nb, h_d)
    return (y_prompt, y_sample,
            heads(ka, B, T, h_a), heads(va, B, T, h_a), p_s_b[None],
            heads(kc, B, T, h_c), heads(vc, B, T, h_c), p_lf_c[None],
            unpair_c(p_c, B), unpair_n(p_n, B), unpair_m(p_m, B),
            heads(ska, S, 1, h_a), heads(sva, S, 1, h_a), s_s_b[None],
            heads(skc, S, 1, h_c), heads(svc, S, 1, h_c), s_lf_c[None],
            unpair_c(s_c, S), unpair_n(s_n, S), unpair_m(s_m, S))
```

```python
import functools
import math

import jax
import jax.numpy as jnp
from jax import lax
from jax.experimental import pallas as pl
from jax.experimental.pallas import tpu as pltpu

F32 = jnp.float32
BF16 = jnp.bfloat16
EPS = 1e-6
NEG = -1e30
LOG2E = math.log2(math.e)
ZERO_PROB_LOG = 110.0

LANES = 128
HALF = 64
NUM_BUCKETS = 32
MAX_EXACT = NUM_BUCKETS // 2
MAX_DISTANCE = 128
CHUNK = 64
SUB = 16
VMEM_LIMIT_BYTES = 56 * 1024 * 1024


def _params(*sem):
    return pltpu.CompilerParams(dimension_semantics=sem, vmem_limit_bytes=VMEM_LIMIT_BYTES)


def _dot(a, b):
    return jnp.dot(a, b, preferred_element_type=F32)


def _dot_nt(a, b):
    return lax.dot_general(a, b, (((1,), (1,)), ((), ())), preferred_element_type=F32)


def _dot_tn(a, b):
    return lax.dot_general(a, b, (((0,), (0,)), ((), ())), preferred_element_type=F32)


def _split3(x):
    hi = x.astype(BF16)
    r = x - hi.astype(F32)
    mid = r.astype(BF16)
    lo = (r - mid.astype(F32)).astype(BF16)
    return hi, mid, lo


def _exact_dot(onesmat, x):
    hi, mid, lo = _split3(x)
    return _dot(onesmat, hi) + _dot(onesmat, mid) + _dot(onesmat, lo)


def _exact_dot_tn(x, onesmat):
    hi, mid, lo = _split3(x)
    return _dot_tn(hi, onesmat) + _dot_tn(mid, onesmat) + _dot_tn(lo, onesmat)


def _exact_dot_r(x, onesmat):
    hi, mid, lo = _split3(x)
    return _dot(hi, onesmat) + _dot(mid, onesmat) + _dot(lo, onesmat)


def _cumsum_rows(x):
    n = x.shape[0]
    row = lax.broadcasted_iota(jnp.int32, x.shape, 0)
    shift = 1
    while shift < n:
        x = x + jnp.where(row >= shift, pltpu.roll(x, shift, 0), 0.0)
        shift *= 2
    return x


def _tri(n, lower):
    r = lax.broadcasted_iota(jnp.int32, (n, n), 0)
    c = lax.broadcasted_iota(jnp.int32, (n, n), 1)
    keep = (c <= r) if lower else (r <= c)
    return jnp.where(keep, 1.0, 0.0).astype(BF16)


def _eye(n):
    r = lax.broadcasted_iota(jnp.int32, (n, n), 0)
    c = lax.broadcasted_iota(jnp.int32, (n, n), 1)
    return jnp.where(r == c, 1.0, 0.0).astype(BF16)


def _log_sigmoid(x):
    return jnp.minimum(x, 0.0) - jnp.log1p(jnp.exp(-jnp.abs(x)))


def _sigmoid(x):
    return 1.0 / (1.0 + jnp.exp(-x))


def _rms(x, g):
    return x * lax.rsqrt(jnp.mean(x * x, axis=-1, keepdims=True) + EPS) * g


def _norm_mm_kernel(x_ref, g_ref, w_ref, *o_refs, tn):
    h = _rms(x_ref[...], g_ref[...]).astype(BF16)
    for j, o_ref in enumerate(o_refs):
        o_ref[...] = _dot(h, w_ref[:, j * tn:(j + 1) * tn])


def _norm_matmul(x, g, w, *, tm, tn):
    m, d = x.shape
    n = w.shape[1]
    nj = n // tn
    return pl.pallas_call(
        functools.partial(_norm_mm_kernel, tn=tn),
        grid=(m // tm,),
        in_specs=[pl.BlockSpec((tm, d), lambda i: (i, 0)),
                  pl.BlockSpec((1, d), lambda i: (0, 0)),
                  pl.BlockSpec((d, n), lambda i: (0, 0))],
        out_specs=[pl.BlockSpec((tm, tn), lambda i: (i, 0))] * nj,
        out_shape=[jax.ShapeDtypeStruct((m, tn), F32)] * nj,
        compiler_params=_params("parallel"),
        name="norm_matmul",
    )(x, g.reshape(1, d), w)


def _mlp_kernel(x_ref, g_ref, wu_ref, wd_ref, *rest, tf, final_norm):
    o_ref = rest[-1]
    x = x_ref[...]
    h = _rms(x, g_ref[...]).astype(BF16)
    acc = x
    for c in range(wu_ref.shape[1] // tf):
        u = _dot(h, wu_ref[:, c * tf:(c + 1) * tf])
        u = jnp.square(jnp.maximum(u, 0.0)).astype(BF16)
        acc = acc + _dot(u, wd_ref[c * tf:(c + 1) * tf, :])
    if final_norm:
        acc = _rms(acc, rest[0][...])
    o_ref[...] = acc


def _mlp(x, g, w_up, w_down, *, tm, tf=512, final_g=None):
    m, d = x.shape
    f = w_up.shape[1]
    in_specs = [pl.BlockSpec((tm, d), lambda i: (i, 0)),
                pl.BlockSpec((1, d), lambda i: (0, 0)),
                pl.BlockSpec((d, f), lambda i: (0, 0)),
                pl.BlockSpec((f, d), lambda i: (0, 0))]
    args = [x, g.reshape(1, d), w_up, w_down]
    if final_g is not None:
        in_specs.append(pl.BlockSpec((1, d), lambda i: (0, 0)))
        args.append(final_g.reshape(1, d))
    return pl.pallas_call(
        functools.partial(_mlp_kernel, tf=tf, final_norm=final_g is not None),
        grid=(m // tm,),
        in_specs=in_specs,
        out_specs=pl.BlockSpec((tm, d), lambda i: (i, 0)),
        out_shape=jax.ShapeDtypeStruct((m, d), F32),
        compiler_params=_params("parallel"),
        name="mlp",
    )(*args)


def _proj_res_kernel(*refs, n_a, ks):
    a_refs, w_ref, res_ref, o_ref = refs[:n_a], refs[n_a], refs[n_a + 1], refs[n_a + 2]
    acc = res_ref[...]
    off = 0
    for a_ref, k in zip(a_refs, ks):
        acc = acc + _dot(a_ref[...], w_ref[off:off + k, :])
        off += k
    o_ref[...] = acc


def _proj_res(a_list, w, res, *, tm):
    m, n = res.shape
    ks = tuple(a.shape[1] for a in a_list)
    in_specs = [pl.BlockSpec((tm, k), lambda i: (i, 0)) for k in ks]
    in_specs += [pl.BlockSpec(w.shape, lambda i: (0, 0)),
                 pl.BlockSpec((tm, n), lambda i: (i, 0))]
    return pl.pallas_call(
        functools.partial(_proj_res_kernel, n_a=len(a_list), ks=ks),
        grid=(m // tm,),
        in_specs=in_specs,
        out_specs=pl.BlockSpec((tm, n), lambda i: (i, 0)),
        out_shape=jax.ShapeDtypeStruct((m, n), F32),
        compiler_params=_params("parallel"),
        name="proj_res",
    )(*a_list, w, res)


def _t5_bias_value(n, tab_ref, h):
    nf = jnp.maximum(n, 1).astype(F32)
    large = MAX_EXACT + (jnp.log(nf / MAX_EXACT) / math.log(MAX_DISTANCE / MAX_EXACT)
                         * (NUM_BUCKETS - MAX_EXACT)).astype(jnp.int32)
    large = jnp.minimum(large, NUM_BUCKETS - 1)
    bucket = jnp.where(n < MAX_EXACT, n, large)
    val = jnp.zeros(n.shape, F32)
    for b in range(NUM_BUCKETS):
        val = jnp.where(bucket == b, tab_ref[b, h], val)
    return val


def _bias_tiles_kernel(tab_ref, o_ref, *, t):
    h, d = pl.program_id(0), pl.program_id(1)
    i = lax.broadcasted_iota(jnp.int32, (t, t), 0)
    j = lax.broadcasted_iota(jnp.int32, (t, t), 1)
    o_ref[0, 0] = _t5_bias_value(jnp.maximum(d * t + j - i, 0), tab_ref, h) * LOG2E


def _bias_tiles(rel_bias, *, t):
    nh = rel_bias.shape[1]
    return pl.pallas_call(
        functools.partial(_bias_tiles_kernel, t=t),
        grid=(nh, 2),
        in_specs=[pl.BlockSpec(memory_space=pltpu.SMEM)],
        out_specs=pl.BlockSpec((1, 1, t, t), lambda h, d: (h, d, 0, 0)),
        out_shape=jax.ShapeDtypeStruct((nh, 2, t, t), F32),
        compiler_params=_params("arbitrary", "arbitrary"),
        name="t5_bias_tiles",
    )(rel_bias)


def _lam_value(lq1, lk1, lq2, lk2, lam_init):
    s1 = jnp.sum(lq1[...] * lk1[...], axis=-1, keepdims=True)
    s2 = jnp.sum(lq2[...] * lk2[...], axis=-1, keepdims=True)
    return jnp.exp(s1) - jnp.exp(s2) + lam_init


def _flash2_kernel(*refs, mode, t, lam_init):
    if mode == "diff":
        (q_ref, k_ref, v_ref, bias_ref, tab_ref, lq1, lk1, lq2, lk2, subln_ref,
         o_ref, m_scr, l_scr, acc_scr, kb_scr, vt_scr) = refs
    else:
        (q_ref, k_ref, v_ref, fq_ref, fk_ref, fend_ref, o_ref, m_scr, l_scr, acc_scr, kb_scr, vt_scr,
         kn_scr) = refs
    hg, qi = pl.program_id(1), pl.program_id(2)
    n_blk = k_ref.shape[1] // t
    lane = lax.broadcasted_iota(jnp.int32, (t, LANES), 1)

    def head_sq_norm_max(x):
        sq = x * x
        return [jnp.max(jnp.sum(jnp.where(lane < HALF, sq, 0.0), axis=-1, keepdims=True), axis=0, keepdims=True),
                jnp.max(jnp.sum(jnp.where(lane < HALF, 0.0, sq), axis=-1, keepdims=True), axis=0, keepdims=True)]

    @pl.when(qi == 0)
    def _():
        kmax = [jnp.zeros((1, 1), F32), jnp.zeros((1, 1), F32)]
        for kb in range(n_blk):
            kblk = k_ref[0, kb * t:(kb + 1) * t, :]
            kb_scr[kb * t:(kb + 1) * t, :] = kblk.astype(BF16)
            vt_scr[:, kb * t:(kb + 1) * t] = v_ref[0, kb * t:(kb + 1) * t, :].T.astype(BF16)
            if mode == "fox":
                kmax = [jnp.maximum(a, b) for a, b in zip(kmax, head_sq_norm_max(kblk))]
        if mode == "fox":
            nl = lax.broadcasted_iota(jnp.int32, (1, LANES), 1)
            kn_scr[...] = jnp.where(nl == 0, kmax[0], jnp.where(nl == 1, kmax[1], 0.0))

    q = q_ref[0] * (HALF ** -0.5 * LOG2E)
    qs = (jnp.where(lane < HALF, q, 0.0).astype(BF16),
          jnp.where(lane >= HALF, q, 0.0).astype(BF16))
    m_scr[...] = jnp.full(m_scr.shape, NEG, F32)
    l_scr[...] = jnp.zeros(l_scr.shape, F32)
    acc_scr[...] = jnp.zeros(acc_scr.shape, F32)
    if mode == "fox":
        fq = fq_ref[0]
        fq_row = lax.broadcasted_iota(jnp.int32, fq.shape, 0)
        fq_raw = [jnp.sum(jnp.where(fq_row == 2 * hg + j, fq, 0.0), axis=0, keepdims=True) for j in range(2)]
        fq_rows = [r * LOG2E for r in fq_raw]

        fend = fend_ref[0]
        fe_row = lax.broadcasted_iota(jnp.int32, fend.shape, 0)
        qmax = head_sq_norm_max(q_ref[0])
        nl = lax.broadcasted_iota(jnp.int32, (1, LANES), 1)
        blk = lax.broadcasted_iota(jnp.int32, (1, n_blk), 1)
        skip = blk < qi
        for j in range(2):
            kmax = jnp.sum(jnp.where(nl == j, kn_scr[...], 0.0), axis=-1, keepdims=True)
            reach = 2.0 * jnp.sqrt(qmax[j] * kmax) * (HALF ** -0.5)
            fend_j = jnp.sum(jnp.where(fe_row == 2 * hg + j, fend, 0.0), axis=0, keepdims=True)
            skip = skip & (reach + fq_raw[j][:, 0:1] - fend_j <= -ZERO_PROB_LOG)
        first_blk = jnp.sum(skip.astype(jnp.int32))

    def step(kb, kind):
        start = pl.multiple_of(kb * t, t)
        k = kb_scr[pl.ds(start, t), :]
        vt = vt_scr[:, pl.ds(start, t)]
        for j in range(2):
            s = _dot_nt(k, qs[j])
            if mode == "diff":
                if kind == "far":
                    shift = tab_ref[NUM_BUCKETS - 1, hg] * LOG2E
                else:
                    shift = 0.0
                    s = s + bias_ref[0, 0 if kind == "diag" else 1]
            else:
                fk = fk_ref[0, pl.ds(start, t), :]
                fk_col = jnp.sum(jnp.where(lax.broadcasted_iota(jnp.int32, fk.shape, 1) == 2 * hg + j, fk, 0.0),
                                 axis=-1, keepdims=True)
                s = s - fk_col * LOG2E
                shift = fq_rows[j]
            if kind == "diag":
                r = lax.broadcasted_iota(jnp.int32, (t, t), 0)
                c = lax.broadcasted_iota(jnp.int32, (t, t), 1)
                s = jnp.where(r <= c, s, NEG)
            m_prev = m_scr[j]
            m_new = jnp.maximum(m_prev, jnp.max(s, axis=0, keepdims=True) + shift)
            alpha = jnp.exp2(m_prev - m_new)
            p = jnp.exp2(s - (m_new - shift))
            l_scr[j] = alpha * l_scr[j] + jnp.sum(p, axis=0, keepdims=True)
            acc_scr[j] = alpha * acc_scr[j] + _dot(vt, p.astype(BF16))
            m_scr[j] = m_new

    if mode == "diff":
        def far_body(kb, carry):
            step(kb, "far")
            return carry
        lax.fori_loop(0, qi - 1, far_body, 0)

        @pl.when(qi >= 1)
        def _():
            step(qi - 1, "sub")
    else:
        def far_body(kb, carry):
            step(kb, "far")
            return carry
        lax.fori_loop(first_blk, qi, far_body, 0)
    step(qi, "diag")

    o0 = acc_scr[0] / l_scr[0]
    o1 = acc_scr[1] / l_scr[1]
    if mode == "diff":
        lam = _lam_value(lq1, lk1, lq2, lk2, lam_init)
        o = (o0 - lam * o1).T
        o_ref[0] = (_rms(o, subln_ref[...]) * (1.0 - lam_init)).astype(o_ref.dtype)
    else:
        row = lax.broadcasted_iota(jnp.int32, (LANES, t), 0)
        o_ref[0] = jnp.where(row < HALF, o0, o1).T.astype(o_ref.dtype)


def _flash2(mode, q, k, v, extra, *, t, lam_init=0.0):
    b, tt, w = q.shape
    ng = w // LANES
    qspec = pl.BlockSpec((1, t, LANES), lambda bi, g, qi: (bi, qi, g))
    kvspec = pl.BlockSpec((1, tt, LANES), lambda bi, g, qi: (bi, 0, g))
    small = lambda shape: pl.BlockSpec(shape, lambda bi, g, qi: (0,) * len(shape))
    if mode == "diff":
        bias, tab, lq1, lk1, lq2, lk2, subln = extra
        in_specs = [qspec, kvspec, kvspec,
                    pl.BlockSpec((1, 2, t, t), lambda bi, g, qi: (g, 0, 0, 0)),
                    pl.BlockSpec(memory_space=pltpu.SMEM),
                    small((1, HALF)), small((1, HALF)), small((1, HALF)), small((1, HALF)),
                    small((1, LANES))]
        args = (q, k, v, bias, tab, lq1, lk1, lq2, lk2, subln)
    else:
        fcol, frow = extra
        nh = fcol.shape[-1]
        fend = frow[:, :, t - 1::t]
        in_specs = [qspec, kvspec, kvspec,
                    pl.BlockSpec((1, nh, t), lambda bi, g, qi: (bi, 0, qi)),
                    pl.BlockSpec((1, tt, nh), lambda bi, g, qi: (bi, 0, 0)),
                    pl.BlockSpec((1, nh, tt // t), lambda bi, g, qi: (bi, 0, 0))]
        args = (q, k, v, frow, fcol, fend)
    scratch = [pltpu.VMEM((2, 1, t), F32), pltpu.VMEM((2, 1, t), F32), pltpu.VMEM((2, LANES, t), F32),
               pltpu.VMEM((tt, LANES), BF16), pltpu.VMEM((LANES, tt), BF16)]
    if mode == "fox":
        scratch.append(pltpu.VMEM((1, LANES), F32))
    return pl.pallas_call(
        functools.partial(_flash2_kernel, mode=mode, t=t, lam_init=lam_init),
        grid=(b, ng, tt // t),
        in_specs=in_specs,
        out_specs=qspec,
        out_shape=jax.ShapeDtypeStruct((b, tt, w), BF16),
        scratch_shapes=scratch,
        compiler_params=_params("parallel", "parallel", "arbitrary"),
        name="flash2_" + mode,
    )(*args)


def _fox_gates_kernel(g_ref, bias_ref, lf_ref, fcol_ref, frow_ref, crow_scr, ccol_scr, *, tc, nh):
    @pl.when(pl.program_id(1) == 0)
    def _():
        crow_scr[...] = jnp.zeros(crow_scr.shape, F32)
        ccol_scr[...] = jnp.zeros(ccol_scr.shape, F32)

    lf = _log_sigmoid(g_ref[0] + bias_ref[...])
    lf_ref[0] = lf[:, :nh]
    fcol = _exact_dot(_tri(tc, True), lf) + crow_scr[...]
    frow = _exact_dot_tn(lf, _tri(tc, False)) + ccol_scr[...]
    fcol_ref[0] = fcol[:, :nh]
    frow_ref[0] = frow[:nh, :]
    crow_scr[...] = fcol[tc - 1:tc, :]
    ccol_scr[...] = frow[:, tc - 1:tc]


def _fox_gates(gates, bias_row, *, tc, nh):
    b, tt, _ = gates.shape
    return pl.pallas_call(
        functools.partial(_fox_gates_kernel, tc=tc, nh=nh),
        grid=(b, tt // tc),
        in_specs=[pl.BlockSpec((1, tc, LANES), lambda bi, c: (bi, c, 0)),
                  pl.BlockSpec((1, LANES), lambda bi, c: (0, 0))],
        out_specs=[pl.BlockSpec((1, tc, nh), lambda bi, c: (bi, c, 0)),
                   pl.BlockSpec((1, tc, nh), lambda bi, c: (bi, c, 0)),
                   pl.BlockSpec((1, nh, tc), lambda bi, c: (bi, 0, c))],
        out_shape=[jax.ShapeDtypeStruct((b, tt, nh), F32),
                   jax.ShapeDtypeStruct((b, tt, nh), F32),
                   jax.ShapeDtypeStruct((b, nh, tt), F32)],
        scratch_shapes=[pltpu.VMEM((1, LANES), F32), pltpu.VMEM((LANES, 1), F32)],
        compiler_params=_params("parallel", "arbitrary"),
        name="fox_gates",
    )(gates, bias_row)


def _hgrn2_kernel(*refs, layer, n_layers, length, n_chunks, bb, t_valid, has_state):
    if has_state:
        q_ref, f_ref, i_ref, g_ref, lbp_ref, gn_ref, s0_ref, o_ref, s_ref, st_scr = refs
    else:
        q_ref, f_ref, i_ref, g_ref, lbp_ref, gn_ref, o_ref, s_ref, st_scr = refs
    tb = pl.program_id(2)

    @pl.when(tb == 0)
    def _():
        for bi in range(bb):
            if has_state:
                st_scr[bi] = s0_ref[bi, 0].T
            else:
                st_scr[bi] = jnp.zeros(st_scr.shape[1:], F32)

    lbp = lbp_ref[...]
    e = jnp.exp(lbp - jnp.max(lbp, axis=0, keepdims=True))
    sm = e / jnp.sum(e, axis=0, keepdims=True)
    lb = jnp.sum(sm[:layer + 1], axis=0, keepdims=True)

    L = length
    C = min(SUB, L)
    tril = _tri(L, True)
    row_c = lax.broadcasted_iota(jnp.int32, (C, LANES), 0)

    states = [st_scr[bi] for bi in range(bb)]
    for c, bi in [(c, bi) for c in range(n_chunks) for bi in range(bb)]:
        rows = slice(c * L, (c + 1) * L)
        f = lb + (1.0 - lb) * _sigmoid(f_ref[bi, rows, :])
        logf = jnp.log(f)
        kk = 1.0 - f
        if t_valid is not None:
            valid = lax.broadcasted_iota(jnp.int32, (L, LANES), 0) < t_valid
            logf = jnp.where(valid, logf, 0.0)
            kk = jnp.where(valid, kk, 0.0)
        q = q_ref[bi, rows, :] * (LANES ** -0.5)
        iv = i_ref[bi, rows, :]
        bcum = _cumsum_rows(logf)
        st = states[bi]
        o_inter = _dot_nt((q * jnp.exp(bcum)).astype(BF16), st.astype(BF16))
        for sc in range(L // C):
            lo = sc * C
            q_s, b_s = q[lo:lo + C], bcum[lo:lo + C]
            o_s = o_inter[lo:lo + C]
            if sc > 0:
                ref_row = bcum[lo - 1:lo]
                qt = q_s * jnp.exp(b_s - ref_row)
                kt = kk[:lo] * jnp.exp(ref_row - bcum[:lo])
                a = _dot_nt(qt.astype(BF16), kt.astype(BF16))
                o_s = o_s + _dot(a.astype(BF16), iv[:lo].astype(BF16))
            b2_s = b_s * LOG2E
            for s in range(C):
                dec = jnp.where(row_c >= s, jnp.exp2(b2_s - b2_s[s:s + 1]), 0.0)
                a_col = jnp.sum(q_s * kk[lo + s:lo + s + 1] * dec, axis=-1, keepdims=True)
                o_s = o_s + a_col * iv[lo + s:lo + s + 1]
            gate = g_ref[bi, c * L + lo:c * L + lo + C, :]
            o_ref[bi, c * L + lo:c * L + lo + C, :] = (
                _rms(o_s, gn_ref[...]) * (gate * _sigmoid(gate))).astype(o_ref.dtype)
        b_last = bcum[L - 1:L]
        kd = kk * jnp.exp(b_last - bcum)
        states[bi] = st * jnp.exp(b_last) + _dot_tn(iv.astype(BF16), kd.astype(BF16))
    for bi in range(bb):
        st_scr[bi] = states[bi]

    @pl.when(tb == pl.num_programs(2) - 1)
    def _():
        for bi in range(bb):
            s_ref[bi, 0] = st_scr[bi].T


def _hgrn2(q, f, i, g, lb_param, gnorm, s0, *, layer, length, n_chunks, bb=1, t_valid=None):
    b, tt, w = q.shape
    nh = w // LANES
    tb = length * n_chunks
    assert b % bb == 0 and tt % tb == 0, (b, bb, tt, tb)
    xspec = pl.BlockSpec((bb, tb, LANES), lambda bi, h, ti: (bi, ti, h))
    sspec = pl.BlockSpec((bb, 1, LANES, LANES), lambda bi, h, ti: (bi, h, 0, 0))
    nl = lb_param.shape[0]
    in_specs = [xspec, xspec, xspec, xspec,
                pl.BlockSpec((nl, LANES), lambda bi, h, ti: (0, h)),
                pl.BlockSpec((1, LANES), lambda bi, h, ti: (0, 0))]
    args = [q, f, i, g, lb_param, gnorm.reshape(1, LANES)]
    if s0 is not None:
        in_specs.append(sspec)
        args.append(s0)
    return pl.pallas_call(
        functools.partial(_hgrn2_kernel, layer=layer, n_layers=nl - 1, length=length,
                          n_chunks=n_chunks, bb=bb, t_valid=t_valid, has_state=s0 is not None),
        grid=(b // bb, nh, tt // tb),
        in_specs=in_specs,
        out_specs=[xspec, sspec],
        out_shape=[jax.ShapeDtypeStruct((b, tt, w), BF16),
                   jax.ShapeDtypeStruct((b, nh, LANES, LANES), F32)],
        scratch_shapes=[pltpu.VMEM((bb, LANES, LANES), F32)],
        compiler_params=_params("parallel", "parallel", "arbitrary"),
        name="hgrn2",
    )(*args)


def _mlstm_kernel(*refs, length, n_chunks, bb, t_valid, has_state, i_lane0, f_lane0):
    if has_state:
        (q_ref, k_ref, v_ref, og_ref, gt_ref, gb_ref, gn_ref, c0_ref, n0_ref, m0_ref,
         o_ref, c_ref, n_ref, m_ref, c_scr, n_scr, m_scr) = refs
    else:
        (q_ref, k_ref, v_ref, og_ref, gt_ref, gb_ref, gn_ref,
         o_ref, c_ref, n_ref, m_ref, c_scr, n_scr, m_scr) = refs
    hp, tb = pl.program_id(1), pl.program_id(2)

    @pl.when(tb == 0)
    def _():
        top = lax.broadcasted_iota(jnp.int32, (LANES, LANES), 0) < HALF
        for bi in range(bb):
            if has_state:
                c0 = c0_ref[bi, 0]
                c_scr[bi] = jnp.concatenate([jnp.where(top, c0, 0.0), jnp.where(top, 0.0, c0)], axis=1)
                n_scr[bi] = n0_ref[bi, 0]
                m_scr[bi] = m0_ref[bi, 0]
            else:
                c_scr[bi] = jnp.zeros(c_scr.shape[1:], F32)
                n_scr[bi] = jnp.zeros(n_scr.shape[1:], F32)
                m_scr[bi] = jnp.zeros(m_scr.shape[1:], F32)

    L = length
    lane = lax.broadcasted_iota(jnp.int32, (L, LANES), 1)
    lane2 = lax.broadcasted_iota(jnp.int32, (L, 2 * L), 1)
    row2 = lax.broadcasted_iota(jnp.int32, (L, 2 * L), 0)
    left2 = lane2 < L
    causal2 = jnp.where(left2, lane2, lane2 - L) <= row2
    left_v = lax.broadcasted_iota(jnp.int32, (L, 2 * LANES), 1) < LANES
    left_q = lane < HALF
    nlane = lax.broadcasted_iota(jnp.int32, (1, LANES), 1)
    crow = lax.broadcasted_iota(jnp.int32, (LANES, 2 * LANES), 0)
    ccol = lax.broadcasted_iota(jnp.int32, (LANES, 2 * LANES), 1)
    own_block = (crow < HALF) == (ccol < LANES)
    is_f = (lane >= f_lane0) & (lane < f_lane0 + 4)
    two = lambda cond, a: jnp.where(cond, a[0], a[1])
    cs = [c_scr[bi] for bi in range(bb)]
    ns = [n_scr[bi] for bi in range(bb)]
    ms = [m_scr[bi] for bi in range(bb)]
    for c, bi in [(c, bi) for c in range(n_chunks) for bi in range(bb)]:
        rows = slice(c * L, (c + 1) * L)
        gpre = gt_ref[bi, rows, :] + gb_ref[...]
        x = jnp.where(is_f, _log_sigmoid(gpre), gpre)
        if t_valid is not None:
            valid = lax.broadcasted_iota(jnp.int32, (L, LANES), 0) < t_valid
            x = jnp.where(valid, x, jnp.where(is_f, 0.0, NEG))
        bcol_all = _cumsum_rows(x)
        xt = x.T
        brow_all = bcol_all.T
        srow = lax.broadcasted_iota(jnp.int32, xt.shape, 0)
        pick = lambda a, l: jnp.sum(jnp.where(lane == l, a, 0.0), axis=-1, keepdims=True)
        pick_t = lambda a, l: jnp.sum(jnp.where(srow == l, a, 0.0), axis=0, keepdims=True)
        heads = (2 * hp, 2 * hp + 1)
        b_col = [pick(bcol_all, f_lane0 + h) for h in heads]
        ig_col = [pick(x, i_lane0 + h) for h in heads]
        b_row2 = jnp.concatenate([pick_t(brow_all, f_lane0 + h) for h in heads], axis=1)
        ig_row2 = jnp.concatenate([pick_t(xt, i_lane0 + h) for h in heads], axis=1)
        m_old = [jnp.sum(jnp.where(nlane == j, ms[bi], 0.0), axis=-1, keepdims=True) for j in range(2)]
        d2 = jnp.where(causal2, two(left2, b_col) - b_row2 + ig_row2, NEG)
        inter = [b_col[j] + m_old[j] for j in range(2)]
        m_t = [jnp.maximum(inter[0], jnp.max(jnp.where(left2, d2, NEG), axis=-1, keepdims=True)),
               jnp.maximum(inter[1], jnp.max(jnp.where(left2, NEG, d2), axis=-1, keepdims=True))]
        w_inter = [jnp.exp(inter[j] - m_t[j]) for j in range(2)]
        wmat2 = jnp.exp(d2 - two(left2, m_t))

        q = q_ref[bi, rows, :]
        k = k_ref[bi, rows, :] * (HALF ** -0.5)
        v = v_ref[bi, rows, :]
        qb = q.astype(BF16)
        cst, nst = cs[bi], ns[bi]
        k_bd = jnp.concatenate([jnp.where(left_q, k, 0.0), jnp.where(left_q, 0.0, k)], axis=0)
        v_bd = jnp.concatenate([jnp.where(left_v, v, 0.0), jnp.where(left_v, 0.0, v)], axis=0)
        qk2 = _dot_nt(qb, k_bd.astype(BF16)) * wmat2
        num2 = (two(left_v, w_inter) * _dot(qb, cst.astype(BF16))
                + _dot(qk2.astype(BF16), v_bd.astype(BF16)))
        qn = q * nst
        den = [w_inter[0] * jnp.sum(jnp.where(left_q, qn, 0.0), axis=-1, keepdims=True)
               + jnp.sum(jnp.where(left2, qk2, 0.0), axis=-1, keepdims=True),
               w_inter[1] * jnp.sum(jnp.where(left_q, 0.0, qn), axis=-1, keepdims=True)
               + jnp.sum(jnp.where(left2, 0.0, qk2), axis=-1, keepdims=True)]
        floor = [jnp.maximum(jnp.abs(den[j]), jnp.exp(-m_t[j])) for j in range(2)]
        hd2 = num2 / two(left_v, floor)
        og = og_ref[bi, rows, :]
        for j in range(2):
            cols = slice(j * LANES, (j + 1) * LANES)
            o_ref[bi, rows, cols] = (_rms(hd2[:, cols], gn_ref[...]) * _sigmoid(og[:, cols])).astype(o_ref.dtype)

        m_new = [m_t[j][L - 1:L] for j in range(2)]
        b_last = [b_col[j][L - 1:L] for j in range(2)]
        w_c = [jnp.exp(b_last[j] + m_old[j] - m_new[j]) for j in range(2)]
        w_s = [jnp.exp(b_last[j] - b_col[j] + ig_col[j] - m_new[j]) for j in range(2)]
        ks = k * two(left_q, w_s)
        upd = _dot_tn(ks.astype(BF16), v.astype(BF16))
        cs[bi] = two(crow < HALF, w_c) * cst + jnp.where(own_block, upd, 0.0)
        ns[bi] = two(nlane < HALF, w_c) * nst + jnp.sum(ks, axis=0, keepdims=True)
        ms[bi] = jnp.where(nlane == 0, m_new[0], jnp.where(nlane == 1, m_new[1], ms[bi]))
    for bi in range(bb):
        c_scr[bi], n_scr[bi], m_scr[bi] = cs[bi], ns[bi], ms[bi]

    @pl.when(tb == pl.num_programs(2) - 1)
    def _():
        for bi in range(bb):
            c_ref[bi, 0] = c_scr[bi, :, :LANES] + c_scr[bi, :, LANES:]
            n_ref[bi, 0] = n_scr[bi]
            m_ref[bi, 0] = m_scr[bi]


def _mlstm(qk, v, og, gates, gate_bias, gnorm, state, *, length, n_chunks, i_lane0, f_lane0, bb=1,
           t_valid=None):
    b, tt, _ = qk.shape
    tb = length * n_chunks
    assert b % bb == 0 and tt % tb == 0, (b, bb, tt, tb)
    npair = 2
    qspec = pl.BlockSpec((bb, tb, LANES), lambda bi, hp, ti: (bi, ti, hp))
    kspec = pl.BlockSpec((bb, tb, LANES), lambda bi, hp, ti: (bi, ti, npair + hp))
    vspec = pl.BlockSpec((bb, tb, 2 * LANES), lambda bi, hp, ti: (bi, ti, hp))
    gspec = pl.BlockSpec((bb, tb, LANES), lambda bi, hp, ti: (bi, ti, 0))
    row = pl.BlockSpec((1, LANES), lambda bi, hp, ti: (0, 0))
    cspec = pl.BlockSpec((bb, 1, LANES, LANES), lambda bi, hp, ti: (bi, hp, 0, 0))
    nspec = pl.BlockSpec((bb, 1, 1, LANES), lambda bi, hp, ti: (bi, hp, 0, 0))
    in_specs = [qspec, kspec, vspec, vspec, gspec, row, row]
    args = [qk, qk, v, og, gates, gate_bias, gnorm.reshape(1, LANES)]
    if state is not None:
        in_specs += [cspec, nspec, nspec]
        args += list(state)
    return pl.pallas_call(
        functools.partial(_mlstm_kernel, length=length, n_chunks=n_chunks, bb=bb, t_valid=t_valid,
                          has_state=state is not None, i_lane0=i_lane0, f_lane0=f_lane0),
        grid=(b // bb, npair, tt // tb),
        in_specs=in_specs,
        out_specs=[vspec, cspec, nspec, nspec],
        out_shape=[jax.ShapeDtypeStruct((b, tt, 4 * LANES), BF16),
                   jax.ShapeDtypeStruct((b, npair, LANES, LANES), F32),
                   jax.ShapeDtypeStruct((b, npair, 1, LANES), F32),
                   jax.ShapeDtypeStruct((b, npair, 1, LANES), F32)],
        scratch_shapes=[pltpu.VMEM((bb, LANES, 2 * LANES), F32), pltpu.VMEM((bb, 1, LANES), F32),
                        pltpu.VMEM((bb, 1, LANES), F32)],
        compiler_params=_params("parallel", "parallel", "arbitrary"),
        name="mlstm",
    )(*args)


ROWS = 16


def _pad_rows(x, rows):
    return jnp.concatenate([x, jnp.zeros((rows - x.shape[0], x.shape[1]), x.dtype)], axis=0)


def _decode_diff_kernel(pt_ref, q_ref, kn_ref, vn_ref, tab_ref, lq1, lk1, lq2, lk2, subln_ref, *rest,
                        n_pages, page, n_heads, lam_init):
    del pt_ref
    k_refs, v_refs = rest[:n_pages], rest[n_pages:2 * n_pages]
    o_ref, bias_scr = rest[2 * n_pages], rest[2 * n_pages + 1]
    past = n_pages * page
    pw = page * n_heads
    rr = lax.broadcasted_iota(jnp.int32, (ROWS, pw), 0) % 8
    ll = lax.broadcasted_iota(jnp.int32, (ROWS, pw), 1)
    own = (ll % n_heads == rr) & (rr < n_heads)
    rcol = lax.broadcasted_iota(jnp.int32, (ROWS, 1), 0) % 8

    @pl.when(pl.program_id(0) == 0)
    def _():
        for p in range(n_pages):
            n = past - (p * page + ll // n_heads)
            acc = jnp.zeros((ROWS, pw), F32)
            for h in range(n_heads):
                acc = jnp.where(rr == h, _t5_bias_value(n, tab_ref, h), acc)
            bias_scr[:, p * pw:(p + 1) * pw] = acc

    lane = lax.broadcasted_iota(jnp.int32, (8, LANES), 1)
    q8 = _pad_rows(q_ref[0], 8) * (HALF ** -0.5)
    qm = jnp.concatenate([jnp.where(lane < HALF, q8, 0.0), jnp.where(lane >= HALF, q8, 0.0)], axis=0)
    qb = qm.astype(BF16)
    kn8 = _pad_rows(kn_ref[0], 8)
    bias_new = jnp.zeros((ROWS, 1), F32)
    for h in range(n_heads):
        bias_new = jnp.where(rcol == h, tab_ref[0, h], bias_new)
    s_new = jnp.sum(qm * jnp.concatenate([kn8, kn8], axis=0), axis=-1, keepdims=True) + bias_new
    s_pages = [jnp.where(own, _dot_nt(qb, k_refs[p][...].astype(BF16)) + bias_scr[:, p * pw:(p + 1) * pw], NEG)
               for p in range(n_pages)]
    m = s_new
    for s in s_pages:
        m = jnp.maximum(m, jnp.max(s, axis=-1, keepdims=True))
    e_new = jnp.exp(s_new - m)
    e_pages = [jnp.exp(s - m) for s in s_pages]
    l = e_new
    for e in e_pages:
        l = l + jnp.sum(e, axis=-1, keepdims=True)
    lam = _lam_value(lq1, lk1, lq2, lk2, lam_init)
    inv = 1.0 / l
    comb = lambda a: a[0:8] - lam * a[8:16]
    out = comb(e_new * inv) * _pad_rows(vn_ref[0], 8)
    for p in range(n_pages):
        wgt = _pad_rows(comb(e_pages[p] * inv), ROWS).astype(BF16)
        out = out + _dot(wgt, v_refs[p][...].astype(BF16))[0:8]
    y = _rms(out, subln_ref[...]) * (1.0 - lam_init)
    o_ref[0] = y[0:n_heads].astype(o_ref.dtype)


def _decode_diff(page_table, q, k_new, v_new, k_pool, v_pool, rel_bias, lq1, lk1, lq2, lk2, subln, *, lam_init):
    ns, n_pages = page_table.shape
    n_heads = q.shape[1]
    pw = k_pool.shape[1]
    row = pl.BlockSpec((1, n_heads, LANES), lambda b, pt: (b, 0, 0))
    small = lambda shape: pl.BlockSpec(shape, lambda b, pt: (0,) * len(shape))
    page_spec = lambda p: pl.BlockSpec((None, pw, LANES), lambda b, pt: (pt[b, p], 0, 0))
    in_specs = [row, row, row, pl.BlockSpec(memory_space=pltpu.SMEM),
                small((1, HALF)), small((1, HALF)), small((1, HALF)), small((1, HALF)), small((1, LANES))]
    in_specs += [page_spec(p) for p in range(n_pages)] * 2
    grid_spec = pltpu.PrefetchScalarGridSpec(
        num_scalar_prefetch=1, grid=(ns,), in_specs=in_specs, out_specs=row,
        scratch_shapes=[pltpu.VMEM((ROWS, n_pages * pw), F32)])
    return pl.pallas_call(
        functools.partial(_decode_diff_kernel, n_pages=n_pages, page=pw // n_heads, n_heads=n_heads,
                          lam_init=lam_init),
        grid_spec=grid_spec,
        out_shape=jax.ShapeDtypeStruct((ns, n_heads, LANES), BF16),
        compiler_params=_params("arbitrary"),
        name="decode_diff",
    )(page_table, q, k_new, v_new, rel_bias, lq1, lk1, lq2, lk2, subln,
      *([k_pool] * n_pages), *([v_pool] * n_pages))


def _decode_fox_kernel(pt_ref, q_ref, kn_ref, vn_ref, g_ref, gb_ref, *rest, n_pages, page, n_heads):
    del pt_ref
    f_refs, k_refs, v_refs = rest[:n_pages], rest[n_pages:2 * n_pages], rest[2 * n_pages:3 * n_pages]
    o_ref, lf_ref = rest[3 * n_pages], rest[3 * n_pages + 1]
    w = n_heads * HALF
    r_i = lax.broadcasted_iota(jnp.int32, (ROWS, w), 0)
    lane = lax.broadcasted_iota(jnp.int32, (ROWS, w), 1)
    q = q_ref[0] * (HALF ** -0.5)
    qbig = jnp.where(lane // HALF == r_i, q, 0.0)
    qb = qbig.astype(BF16)
    lf_row = _log_sigmoid(g_ref[0] + gb_ref[...])
    lf_ref[0] = lf_row[:, :n_heads]
    gr = lax.broadcasted_iota(jnp.int32, (ROWS, LANES), 0)
    gl = lax.broadcasted_iota(jnp.int32, (ROWS, LANES), 1)
    lf_col = jnp.sum(jnp.where((gl == gr) & (gr < n_heads), lf_row, 0.0), axis=-1, keepdims=True)
    triu = _tri(page, False)
    carry = jnp.zeros((ROWS, 1), F32)
    prefix = []
    for p in range(n_pages):
        pp = _exact_dot_r(_pad_rows(f_refs[p][...], ROWS), triu) + carry
        prefix.append(pp)
        carry = pp[:, page - 1:page]
    s_new = jnp.sum(qbig * kn_ref[0], axis=-1, keepdims=True)
    s_pages = [_dot(qb, k_refs[p][...].astype(BF16)) + (lf_col + carry - prefix[p])
               for p in range(n_pages)]
    m = s_new
    for s in s_pages:
        m = jnp.maximum(m, jnp.max(s, axis=-1, keepdims=True))
    e_new = jnp.exp(s_new - m)
    e_pages = [jnp.exp(s - m) for s in s_pages]
    l = e_new
    for e in e_pages:
        l = l + jnp.sum(e, axis=-1, keepdims=True)
    inv = 1.0 / l
    out = (e_new * inv) * vn_ref[0]
    for p in range(n_pages):
        out = out + _dot_nt((e_pages[p] * inv).astype(BF16), v_refs[p][...].astype(BF16))
    o_ref[0] = jnp.sum(jnp.where(lane // HALF == r_i, out, 0.0), axis=0, keepdims=True).astype(o_ref.dtype)


def _decode_fox(page_table, q, k_new, v_new, gates, gate_bias, lf_pool_t, k_pool_t, v_pool_t):
    ns, n_pages = page_table.shape
    _, w, page = k_pool_t.shape
    n_heads = w // HALF
    row = pl.BlockSpec((1, 1, w), lambda b, pt: (b, 0, 0))
    grow = pl.BlockSpec((1, 1, LANES), lambda b, pt: (b, 0, 0))
    page_spec = lambda p: pl.BlockSpec((None, w, page), lambda b, pt: (pt[b, p], 0, 0))
    f_spec = lambda p: pl.BlockSpec((None, n_heads, page), lambda b, pt: (pt[b, p], 0, 0))
    in_specs = [row, row, row, grow, pl.BlockSpec((1, LANES), lambda b, pt: (0, 0))]
    in_specs += [f_spec(p) for p in range(n_pages)] + [page_spec(p) for p in range(n_pages)] * 2
    grid_spec = pltpu.PrefetchScalarGridSpec(
        num_scalar_prefetch=1, grid=(ns,), in_specs=in_specs,
        out_specs=[row, pl.BlockSpec((1, 1, n_heads), lambda b, pt: (b, 0, 0))])
    return pl.pallas_call(
        functools.partial(_decode_fox_kernel, n_pages=n_pages, page=page, n_heads=n_heads),
        grid_spec=grid_spec,
        out_shape=[jax.ShapeDtypeStruct((ns, 1, w), BF16), jax.ShapeDtypeStruct((ns, 1, n_heads), F32)],
        compiler_params=_params("arbitrary"),
        name="decode_fox",
    )(page_table, q, k_new, v_new, gates, gate_bias,
      *([lf_pool_t] * n_pages), *([k_pool_t] * n_pages), *([v_pool_t] * n_pages))


def kernel(x_prompt, x_sample, cache_k_a, cache_v_a, state_s_b, cache_k_c, cache_v_c, cache_logf_c,
           state_c_d, state_n_d, state_m_d, page_table, norm_mix, w_in_even, lambda_q1, lambda_k1,
           lambda_q2, lambda_k2, subln_a, rel_bias, lb_param, gnorm_b, w_out_even, w_in_odd, b_f_c,
           b_i_d, b_f_d, gnorm_d, w_out_odd, norm_mlp, w_up, w_down, norm_final):
    B, T, D = x_prompt.shape
    S = x_sample.shape[0]
    n_pool, page = cache_k_a.shape[1], cache_k_a.shape[2]
    h_a, h_c, h_d = cache_k_a.shape[3], cache_k_c.shape[3], state_c_d.shape[2]
    w_a = h_a * LANES
    lam_init = 0.8 - 0.6 * math.exp(-0.3 * 0)
    tq = 512

    w_in0 = w_in_even[0].astype(BF16)
    wo = w_in_odd[0]
    c = [0]
    for sz in (h_c * HALF, h_c * HALF, h_c * HALF, h_c, h_d * HALF, h_d * HALF, h_d * LANES, h_d, h_d,
               h_d * LANES):
        c.append(c[-1] + sz)
    col = lambda i: wo[:, c[i]:c[i + 1]]
    gate_cols = jnp.concatenate([col(3), col(7), col(8)], axis=1)
    gate_cols = jnp.pad(gate_cols, ((0, 0), (0, w_a - gate_cols.shape[1])))
    w_in1 = jnp.concatenate([col(0), col(1), col(2), col(4), col(5), col(6), col(9), gate_cols],
                            axis=1).astype(BF16)
    gate_bias = jnp.pad(jnp.concatenate([b_f_c[0], b_i_d[0], b_f_d[0]]), (0, LANES - h_c - 2 * h_d))
    gate_bias = gate_bias.reshape(1, LANES)
    i_lane0, f_lane0 = h_c, h_c + h_d
    w_out0, w_out1 = w_out_even[0].astype(BF16), w_out_odd[0].astype(BF16)
    w_up_b, w_down_b = w_up.astype(BF16), w_down.astype(BF16)
    lq1, lk1, lq2, lk2 = (a.reshape(1, HALF) for a in (lambda_q1[0], lambda_k1[0], lambda_q2[0], lambda_k2[0]))
    subln = subln_a[0].reshape(1, LANES)

    M = B * T
    xp = x_prompt.reshape(M, D)
    qa, ka, va, qb, fb, ib, gb = _norm_matmul(xp, norm_mix[0], w_in0, tm=512, tn=w_a)
    r3 = lambda a: a.reshape(B, T, w_a)
    bias = _bias_tiles(rel_bias, t=tq)
    o_a = _flash2("diff", r3(qa), r3(ka), r3(va), (bias, rel_bias, lq1, lk1, lq2, lk2, subln),
                  t=tq, lam_init=lam_init)
    o_b, p_s_b = _hgrn2(r3(qb), r3(fb), r3(ib), r3(gb), lb_param, gnorm_b[0], None,
                        layer=0, length=CHUNK, n_chunks=8, bb=2)
    x1 = _proj_res([o_a.reshape(M, w_a), o_b.reshape(M, w_a)], w_out0, xp, tm=512)
    x2 = _mlp(x1, norm_mlp[0], w_up_b[0], w_down_b[0], tm=512)
    qc, kc, vc, qkd, vd, od, gts = _norm_matmul(x2, norm_mix[1], w_in1, tm=512, tn=w_a)
    p_lf_c, fcol, frow = _fox_gates(r3(gts), gate_bias, tc=512, nh=h_c)
    o_c = _flash2("fox", r3(qc), r3(kc), r3(vc), (fcol, frow), t=tq)
    o_d, p_c, p_n, p_m = _mlstm(r3(qkd), r3(vd), r3(od), r3(gts), gate_bias, gnorm_d[0], None,
                                length=CHUNK, n_chunks=4, bb=2, i_lane0=i_lane0, f_lane0=f_lane0)
    x3 = _proj_res([o_c.reshape(M, w_a), o_d.reshape(M, w_a)], w_out1, x2, tm=512)
    y_prompt = _mlp(x3, norm_mlp[1], w_up_b[1], w_down_b[1], tm=512, final_g=norm_final).reshape(B, T, D)

    PADT = 8
    DEC_BB = 8
    xs = x_sample.reshape(S, D)
    sqa, ska, sva, sqb, sfb, sib, sgb = _norm_matmul(xs, norm_mix[0], w_in0, tm=S, tn=w_a)
    s3 = lambda a: a.reshape(S, 1, w_a)
    padt = lambda a: jnp.pad(a.reshape(S, 1, w_a), ((0, 0), (0, PADT - 1), (0, 0)))
    sh = lambda a: a.reshape(S, h_a, LANES)
    so_a = _decode_diff(page_table, sh(sqa), sh(ska), sh(sva),
                        cache_k_a.reshape(n_pool, page * h_a, LANES), cache_v_a.reshape(n_pool, page * h_a, LANES),
                        rel_bias, lq1, lk1, lq2, lk2, subln, lam_init=lam_init)
    so_b, s_s_b = _hgrn2(padt(sqb), padt(sfb), padt(sib), padt(sgb), lb_param, gnorm_b[0], state_s_b[0],
                         layer=0, length=PADT, n_chunks=1, bb=DEC_BB, t_valid=1)
    sx1 = _proj_res([so_a.reshape(S, w_a), so_b[:, 0]], w_out0, xs, tm=S)
    sx2 = _mlp(sx1, norm_mlp[0], w_up_b[0], w_down_b[0], tm=S)
    sqc, skc, svc, sqkd, svd, sod, sgts = _norm_matmul(sx2, norm_mix[1], w_in1, tm=S, tn=w_a)
    lf_pool_t = jnp.swapaxes(cache_logf_c[0], 1, 2)
    pool_t = lambda a: jnp.transpose(a[0], (0, 2, 3, 1)).reshape(n_pool, w_a, page)
    so_c, s_lf_c = _decode_fox(page_table, s3(sqc), s3(skc), s3(svc), sgts[:, :LANES].reshape(S, 1, LANES),
                               gate_bias, lf_pool_t, pool_t(cache_k_c), pool_t(cache_v_c))
    npair = h_d // 2
    m0 = jnp.pad(state_m_d[0].reshape(S, npair, 1, 2), ((0, 0), (0, 0), (0, 0), (0, LANES - 2)))
    so_d, s_c, s_n, s_m = _mlstm(padt(sqkd), padt(svd), padt(sod), padt(sgts), gate_bias, gnorm_d[0],
                                 (state_c_d[0].reshape(S, npair, LANES, LANES),
                                  state_n_d[0].reshape(S, npair, 1, LANES), m0),
                                 length=PADT, n_chunks=1, i_lane0=i_lane0, f_lane0=f_lane0, bb=DEC_BB,
                                 t_valid=1)
    sx3 = _proj_res([so_c.reshape(S, w_a), so_d[:, 0]], w_out1, sx2, tm=S)
    y_sample = _mlp(sx3, norm_mlp[1], w_up_b[1], w_down_b[1], tm=S, final_g=norm_final).reshape(S, 1, D)

    dk_d = state_c_d.shape[3]
    heads = lambda a, nb, tt, nh: a.reshape(1, nb, tt, nh, w_a // nh)
    unpair_c = lambda a, nb: a.reshape(1, nb, h_d, dk_d, LANES)
    unpair_n = lambda a, nb: a.reshape(1, nb, h_d, dk_d)
    unpair_m = lambda a, nb: a[..., :2].reshape(1, nb, h_d)
    return (y_prompt, y_sample,
            heads(ka, B, T, h_a), heads(va, B, T, h_a), p_s_b[None],
            heads(kc, B, T, h_c), heads(vc, B, T, h_c), p_lf_c[None],
            unpair_c(p_c, B), unpair_n(p_n, B), unpair_m(p_m, B),
            heads(ska, S, 1, h_a), heads(sva, S, 1, h_a), s_s_b[None],
            heads(skc, S, 1, h_c), heads(svc, S, 1, h_c), s_lf_c[None],
            unpair_c(s_c, S), unpair_n(s_n, S), unpair_m(s_m, S))
```

```python
import functools
import math

import jax
import jax.numpy as jnp
from jax import lax
from jax.experimental import pallas as pl
from jax.experimental.pallas import tpu as pltpu

F32 = jnp.float32
BF16 = jnp.bfloat16
EPS = 1e-6
NEG = -1e30
LOG2E = math.log2(math.e)
ZERO_PROB_LOG = 110.0

LANES = 128
HALF = 64
NUM_BUCKETS = 32
MAX_EXACT = NUM_BUCKETS // 2
MAX_DISTANCE = 128
CHUNK = 64
SUB = 16
VMEM_LIMIT_BYTES = 56 * 1024 * 1024


def _params(*sem):
    return pltpu.CompilerParams(dimension_semantics=sem, vmem_limit_bytes=VMEM_LIMIT_BYTES)


def _dot(a, b):
    return jnp.dot(a, b, preferred_element_type=F32)


def _dot_nt(a, b):
    return lax.dot_general(a, b, (((1,), (1,)), ((), ())), preferred_element_type=F32)


def _dot_tn(a, b):
    return lax.dot_general(a, b, (((0,), (0,)), ((), ())), preferred_element_type=F32)


def _split3(x):
    hi = x.astype(BF16)
    r = x - hi.astype(F32)
    mid = r.astype(BF16)
    lo = (r - mid.astype(F32)).astype(BF16)
    return hi, mid, lo


def _exact_dot(onesmat, x):
    hi, mid, lo = _split3(x)
    return _dot(onesmat, hi) + _dot(onesmat, mid) + _dot(onesmat, lo)


def _exact_dot_tn(x, onesmat):
    hi, mid, lo = _split3(x)
    return _dot_tn(hi, onesmat) + _dot_tn(mid, onesmat) + _dot_tn(lo, onesmat)


def _exact_dot_r(x, onesmat):
    hi, mid, lo = _split3(x)
    return _dot(hi, onesmat) + _dot(mid, onesmat) + _dot(lo, onesmat)


def _cumsum_rows(x):
    n = x.shape[0]
    row = lax.broadcasted_iota(jnp.int32, x.shape, 0)
    shift = 1
    while shift < n:
        x = x + jnp.where(row >= shift, pltpu.roll(x, shift, 0), 0.0)
        shift *= 2
    return x


def _tri(n, lower):
    r = lax.broadcasted_iota(jnp.int32, (n, n), 0)
    c = lax.broadcasted_iota(jnp.int32, (n, n), 1)
    keep = (c <= r) if lower else (r <= c)
    return jnp.where(keep, 1.0, 0.0).astype(BF16)


def _eye(n):
    r = lax.broadcasted_iota(jnp.int32, (n, n), 0)
    c = lax.broadcasted_iota(jnp.int32, (n, n), 1)
    return jnp.where(r == c, 1.0, 0.0).astype(BF16)


def _log_sigmoid(x):
    return jnp.minimum(x, 0.0) - jnp.log1p(jnp.exp(-jnp.abs(x)))


def _sigmoid(x):
    return 1.0 / (1.0 + jnp.exp(-x))


def _rms(x, g):
    return x * lax.rsqrt(jnp.mean(x * x, axis=-1, keepdims=True) + EPS) * g


def _norm_mm_kernel(x_ref, g_ref, w_ref, *rest, tn, tiles, n_t, head_major):
    h = _rms(x_ref[...], g_ref[...]).astype(BF16)
    if n_t:
        wt_ref, rest = rest[0], rest[1:]
    tm = x_ref.shape[0]
    nh = tn // LANES
    extra = iter(rest[len(tiles):])
    for o_ref, j in zip(rest, tiles):
        z = _dot(h, w_ref[:, j * tn:(j + 1) * tn])
        o_ref[...] = z.astype(o_ref.dtype)
        if j in head_major:
            hm_ref = next(extra)
            for hd in range(nh):
                hm_ref[pl.ds(hd, tm, stride=nh), :] = z[:, hd * LANES:(hd + 1) * LANES]
    for j in range(n_t):
        next(extra)[0] = _dot_nt(wt_ref[j * tn:(j + 1) * tn, :], h)


def _norm_matmul(x, g, w, *, tm, tn, dtypes=None, head_major=(), wt=None, seq=None):
    m, d = x.shape
    n = w.shape[1]
    dtypes = dtypes or [F32] * (n // tn)
    tiles = [j for j, dt in enumerate(dtypes) if dt is not None]
    nh = tn // LANES
    in_specs = [pl.BlockSpec((tm, d), lambda i: (i, 0)),
                pl.BlockSpec((1, d), lambda i: (0, 0)),
                pl.BlockSpec((d, n), lambda i: (0, 0))]
    args = [x, g.reshape(1, d), w]
    out_specs = [pl.BlockSpec((tm, tn), lambda i: (i, 0))] * len(tiles)
    out_shape = [jax.ShapeDtypeStruct((m, tn), dtypes[j]) for j in tiles]
    out_specs += [pl.BlockSpec((tm * nh, LANES), lambda i: (i, 0))] * len(head_major)
    out_shape += [jax.ShapeDtypeStruct((m * nh, LANES), F32)] * len(head_major)
    n_t = 0
    if wt is not None:
        n_t = wt.shape[0] // tn
        per_seq = seq // tm
        in_specs.append(pl.BlockSpec(wt.shape, lambda i: (0, 0)))
        args.append(wt)
        out_specs += [pl.BlockSpec((1, tn, tm), lambda i: (i // per_seq, 0, i % per_seq))] * n_t
        out_shape += [jax.ShapeDtypeStruct((m // seq, tn, seq), F32)] * n_t
    return pl.pallas_call(
        functools.partial(_norm_mm_kernel, tn=tn, tiles=tiles, n_t=n_t, head_major=tuple(head_major)),
        grid=(m // tm,),
        in_specs=in_specs,
        out_specs=out_specs,
        out_shape=out_shape,
        compiler_params=_params("parallel"),
        name="norm_matmul",
    )(*args)


def _mix_mlp_kernel(a1_ref, a2_ref, wo_ref, res_ref, g_ref, wu_ref, wd_ref, *rest, tf, final_norm):
    o_ref = rest[-1]
    x = res_ref[...] + _dot(jnp.concatenate([a1_ref[...], a2_ref[...]], axis=1), wo_ref[...])
    h = _rms(x, g_ref[...]).astype(BF16)
    acc = x
    for c in range(wu_ref.shape[1] // tf):
        u = _dot(h, wu_ref[:, c * tf:(c + 1) * tf])
        u = jnp.square(jnp.maximum(u, 0.0)).astype(BF16)
        acc = acc + _dot(u, wd_ref[c * tf:(c + 1) * tf, :])
    if final_norm:
        acc = _rms(acc, rest[0][...])
    o_ref[...] = acc


def _mix_mlp(a1, a2, w_out, res, g, w_up, w_down, *, tm, tf=512, final_g=None):
    m, d = res.shape
    f = w_up.shape[1]
    whole = lambda a: pl.BlockSpec(a.shape, lambda i: (0, 0))
    rows = lambda a: pl.BlockSpec((tm, a.shape[1]), lambda i: (i, 0))
    g2 = g.reshape(1, d)
    in_specs = [rows(a1), rows(a2), whole(w_out), rows(res), whole(g2), whole(w_up), whole(w_down)]
    args = [a1, a2, w_out, res, g2, w_up, w_down]
    if final_g is not None:
        args.append(final_g.reshape(1, d))
        in_specs.append(whole(args[-1]))
    return pl.pallas_call(
        functools.partial(_mix_mlp_kernel, tf=tf, final_norm=final_g is not None),
        grid=(m // tm,),
        in_specs=in_specs,
        out_specs=pl.BlockSpec((tm, d), lambda i: (i, 0)),
        out_shape=jax.ShapeDtypeStruct((m, d), F32),
        compiler_params=_params("parallel"),
        name="mix_mlp",
    )(*args)


def _t5_bias_value(n, tab_ref, h):
    val = jnp.zeros(n.shape, F32)
    for b in range(MAX_EXACT):
        val = jnp.where(n == b, tab_ref[b, h], val)
    n_log = NUM_BUCKETS - MAX_EXACT
    for k in range(n_log):
        start = math.ceil(MAX_EXACT * (MAX_DISTANCE / MAX_EXACT) ** (k / n_log))
        val = jnp.where(n >= start, tab_ref[MAX_EXACT + k, h], val)
    return val


def _bias_tiles_kernel(tab_ref, o_ref, *, t):
    h, d = pl.program_id(0), pl.program_id(1)
    i = lax.broadcasted_iota(jnp.int32, (t, t), 0)
    j = lax.broadcasted_iota(jnp.int32, (t, t), 1)
    o_ref[0, 0] = _t5_bias_value(jnp.maximum(d * t + j - i, 0), tab_ref, h) * LOG2E


def _bias_tiles(rel_bias, *, t):
    nh = rel_bias.shape[1]
    return pl.pallas_call(
        functools.partial(_bias_tiles_kernel, t=t),
        grid=(nh, 2),
        in_specs=[pl.BlockSpec(memory_space=pltpu.SMEM)],
        out_specs=pl.BlockSpec((1, 1, t, t), lambda h, d: (h, d, 0, 0)),
        out_shape=jax.ShapeDtypeStruct((nh, 2, t, t), F32),
        compiler_params=_params("arbitrary", "arbitrary"),
        name="t5_bias_tiles",
    )(rel_bias)


def _lam_value(lq1, lk1, lq2, lk2, lam_init):
    s1 = jnp.sum(lq1[...] * lk1[...], axis=-1, keepdims=True)
    s2 = jnp.sum(lq2[...] * lk2[...], axis=-1, keepdims=True)
    return jnp.exp(s1) - jnp.exp(s2) + lam_init


def _flash2_kernel(*refs, mode, t, lam_init, v_transposed):
    if mode == "diff":
        (q_ref, k_ref, v_ref, bias_ref, tab_ref, lq1, lk1, lq2, lk2, subln_ref,
         o_ref, m_scr, l_scr, acc_scr, kb_scr, vt_scr) = refs
    else:
        (q_ref, k_ref, v_ref, fq_ref, fk_ref, fend_ref, o_ref, m_scr, l_scr, acc_scr, kb_scr, vt_scr,
         kn_scr) = refs
    hg, qi = pl.program_id(1), pl.program_id(2)
    n_blk = k_ref.shape[1] // t
    lane = lax.broadcasted_iota(jnp.int32, (t, LANES), 1)

    def head_sq_norm_max(x):
        sq = x * x
        return [jnp.max(jnp.sum(jnp.where(lane < HALF, sq, 0.0), axis=-1, keepdims=True), axis=0, keepdims=True),
                jnp.max(jnp.sum(jnp.where(lane < HALF, 0.0, sq), axis=-1, keepdims=True), axis=0, keepdims=True)]

    @pl.when(qi == 0)
    def _():
        kmax = [jnp.zeros((1, 1), F32), jnp.zeros((1, 1), F32)]
        for kb in range(n_blk):
            kblk = k_ref[0, kb * t:(kb + 1) * t, :]
            kb_scr[kb * t:(kb + 1) * t, :] = kblk.astype(BF16)
            if v_transposed:
                vt_scr[:, kb * t:(kb + 1) * t] = v_ref[0, :, kb * t:(kb + 1) * t].astype(BF16)
            else:
                vt_scr[:, kb * t:(kb + 1) * t] = v_ref[0, kb * t:(kb + 1) * t, :].T.astype(BF16)
            if mode == "fox":
                kmax = [jnp.maximum(a, b) for a, b in zip(kmax, head_sq_norm_max(kblk.astype(F32)))]
        if mode == "fox":
            nl = lax.broadcasted_iota(jnp.int32, (1, LANES), 1)
            kn_scr[...] = jnp.where(nl == 0, kmax[0], jnp.where(nl == 1, kmax[1], 0.0))

    q = q_ref[0] * (HALF ** -0.5 * LOG2E)
    qs = (jnp.where(lane < HALF, q, 0.0).astype(BF16),
          jnp.where(lane >= HALF, q, 0.0).astype(BF16))
    m_scr[...] = jnp.full(m_scr.shape, NEG, F32)
    l_scr[...] = jnp.zeros(l_scr.shape, F32)
    acc_scr[...] = jnp.zeros(acc_scr.shape, F32)
    if mode == "fox":
        fq = fq_ref[0]
        fq_row = lax.broadcasted_iota(jnp.int32, fq.shape, 0)
        fq_raw = [jnp.sum(jnp.where(fq_row == 2 * hg + j, fq, 0.0), axis=0, keepdims=True) for j in range(2)]
        fq_rows = [r * LOG2E for r in fq_raw]

        fend = fend_ref[0]
        fe_row = lax.broadcasted_iota(jnp.int32, fend.shape, 0)
        qmax = head_sq_norm_max(q_ref[0])
        nl = lax.broadcasted_iota(jnp.int32, (1, LANES), 1)
        blk = lax.broadcasted_iota(jnp.int32, (1, n_blk), 1)
        skip = blk < qi
        for j in range(2):
            kmax = jnp.sum(jnp.where(nl == j, kn_scr[...], 0.0), axis=-1, keepdims=True)
            reach = 2.0 * jnp.sqrt(qmax[j] * kmax) * (HALF ** -0.5)
            fend_j = jnp.sum(jnp.where(fe_row == 2 * hg + j, fend, 0.0), axis=0, keepdims=True)
            skip = skip & (reach + fq_raw[j][:, 0:1] - fend_j <= -ZERO_PROB_LOG)
        first_blk = jnp.sum(skip.astype(jnp.int32))

    def step(kb, kind):
        start = pl.multiple_of(kb * t, t)
        k = kb_scr[pl.ds(start, t), :]
        vt = vt_scr[:, pl.ds(start, t)]
        for j in range(2):
            s = _dot_nt(k, qs[j])
            if mode == "diff":
                if kind == "far":
                    shift = tab_ref[NUM_BUCKETS - 1, hg] * LOG2E
                else:
                    shift = 0.0
                    s = s + bias_ref[0, 0 if kind == "diag" else 1]
            else:
                fk = fk_ref[0, pl.ds(start, t), :]
                fk_col = jnp.sum(jnp.where(lax.broadcasted_iota(jnp.int32, fk.shape, 1) == 2 * hg + j, fk, 0.0),
                                 axis=-1, keepdims=True)
                s = s - fk_col * LOG2E
                shift = fq_rows[j]
            if kind == "diag":
                r = lax.broadcasted_iota(jnp.int32, (t, t), 0)
                c = lax.broadcasted_iota(jnp.int32, (t, t), 1)
                s = jnp.where(r <= c, s, NEG)
            m_prev = m_scr[j]
            m_new = jnp.maximum(m_prev, jnp.max(s, axis=0, keepdims=True) + shift)
            alpha = jnp.exp2(m_prev - m_new)
            p = jnp.exp2(s - (m_new - shift))
            l_scr[j] = alpha * l_scr[j] + jnp.sum(p, axis=0, keepdims=True)
            acc_scr[j] = alpha * acc_scr[j] + _dot(vt, p.astype(BF16))
            m_scr[j] = m_new

    if mode == "diff":
        def far_body(kb, carry):
            step(kb, "far")
            return carry
        lax.fori_loop(0, qi - 1, far_body, 0)

        @pl.when(qi >= 1)
        def _():
            step(qi - 1, "sub")
    else:
        def far_body(kb, carry):
            step(kb, "far")
            return carry
        lax.fori_loop(first_blk, qi, far_body, 0)
    step(qi, "diag")

    o0 = acc_scr[0] / l_scr[0]
    o1 = acc_scr[1] / l_scr[1]
    if mode == "diff":
        lam = _lam_value(lq1, lk1, lq2, lk2, lam_init)
        o = (o0 - lam * o1).T
        o_ref[0] = (_rms(o, subln_ref[...]) * (1.0 - lam_init)).astype(o_ref.dtype)
    else:
        row = lax.broadcasted_iota(jnp.int32, (LANES, t), 0)
        o_ref[0] = jnp.where(row < HALF, o0, o1).T.astype(o_ref.dtype)


def _flash2(mode, q, k, v, extra, *, t, lam_init=0.0, v_transposed=False):
    b, tt, w = q.shape
    ng = w // LANES
    qspec = pl.BlockSpec((1, t, LANES), lambda bi, g, qi: (bi, qi, g))
    kvspec = pl.BlockSpec((1, tt, LANES), lambda bi, g, qi: (bi, 0, g))
    vspec = pl.BlockSpec((1, LANES, tt), lambda bi, g, qi: (bi, g, 0)) if v_transposed else kvspec
    small = lambda shape: pl.BlockSpec(shape, lambda bi, g, qi: (0,) * len(shape))
    if mode == "diff":
        bias, tab, lq1, lk1, lq2, lk2, subln = extra
        in_specs = [qspec, kvspec, vspec,
                    pl.BlockSpec((1, 2, t, t), lambda bi, g, qi: (g, 0, 0, 0)),
                    pl.BlockSpec(memory_space=pltpu.SMEM),
                    small((1, HALF)), small((1, HALF)), small((1, HALF)), small((1, HALF)),
                    small((1, LANES))]
        args = (q, k, v, bias, tab, lq1, lk1, lq2, lk2, subln)
    else:
        fcol, frow = extra
        nh = fcol.shape[-1]
        fend = frow[:, :, t - 1::t]
        in_specs = [qspec, kvspec, vspec,
                    pl.BlockSpec((1, nh, t), lambda bi, g, qi: (bi, 0, qi)),
                    pl.BlockSpec((1, tt, nh), lambda bi, g, qi: (bi, 0, 0)),
                    pl.BlockSpec((1, nh, tt // t), lambda bi, g, qi: (bi, 0, 0))]
        args = (q, k, v, frow, fcol, fend)
    scratch = [pltpu.VMEM((2, 1, t), F32), pltpu.VMEM((2, 1, t), F32), pltpu.VMEM((2, LANES, t), F32),
               pltpu.VMEM((tt, LANES), BF16), pltpu.VMEM((LANES, tt), BF16)]
    if mode == "fox":
        scratch.append(pltpu.VMEM((1, LANES), F32))
    return pl.pallas_call(
        functools.partial(_flash2_kernel, mode=mode, t=t, lam_init=lam_init, v_transposed=v_transposed),
        grid=(b, ng, tt // t),
        in_specs=in_specs,
        out_specs=qspec,
        out_shape=jax.ShapeDtypeStruct((b, tt, w), BF16),
        scratch_shapes=scratch,
        compiler_params=_params("parallel", "parallel", "arbitrary"),
        name="flash2_" + mode,
    )(*args)


def _fox_gates_kernel(g_ref, bias_ref, lf_ref, fcol_ref, frow_ref, crow_scr, ccol_scr, *, tc, nh):
    @pl.when(pl.program_id(1) == 0)
    def _():
        crow_scr[...] = jnp.zeros(crow_scr.shape, F32)
        ccol_scr[...] = jnp.zeros(ccol_scr.shape, F32)

    lf = _log_sigmoid(g_ref[0] + bias_ref[...])
    lf_ref[0] = lf[:, :nh]
    fcol = _exact_dot(_tri(tc, True), lf) + crow_scr[...]
    frow = _exact_dot_tn(lf, _tri(tc, False)) + ccol_scr[...]
    fcol_ref[0] = fcol[:, :nh]
    frow_ref[0] = frow[:nh, :]
    crow_scr[...] = fcol[tc - 1:tc, :]
    ccol_scr[...] = frow[:, tc - 1:tc]


def _fox_gates(gates, bias_row, *, tc, nh):
    b, tt, _ = gates.shape
    return pl.pallas_call(
        functools.partial(_fox_gates_kernel, tc=tc, nh=nh),
        grid=(b, tt // tc),
        in_specs=[pl.BlockSpec((1, tc, LANES), lambda bi, c: (bi, c, 0)),
                  pl.BlockSpec((1, LANES), lambda bi, c: (0, 0))],
        out_specs=[pl.BlockSpec((1, tc, nh), lambda bi, c: (bi, c, 0)),
                   pl.BlockSpec((1, tc, nh), lambda bi, c: (bi, c, 0)),
                   pl.BlockSpec((1, nh, tc), lambda bi, c: (bi, 0, c))],
        out_shape=[jax.ShapeDtypeStruct((b, tt, nh), F32),
                   jax.ShapeDtypeStruct((b, tt, nh), F32),
                   jax.ShapeDtypeStruct((b, nh, tt), F32)],
        scratch_shapes=[pltpu.VMEM((1, LANES), F32), pltpu.VMEM((LANES, 1), F32)],
        compiler_params=_params("parallel", "arbitrary"),
        name="fox_gates",
    )(gates, bias_row)


def _hgrn2_kernel(*refs, layer, n_layers, length, n_chunks, bb, t_valid, has_state):
    if has_state:
        q_ref, f_ref, i_ref, g_ref, lbp_ref, gn_ref, s0_ref, o_ref, s_ref, st_scr = refs
    else:
        q_ref, f_ref, i_ref, g_ref, lbp_ref, gn_ref, o_ref, s_ref, st_scr = refs
    tb = pl.program_id(2)

    @pl.when(tb == 0)
    def _():
        for bi in range(bb):
            if has_state:
                st_scr[bi] = s0_ref[bi, 0].T
            else:
                st_scr[bi] = jnp.zeros(st_scr.shape[1:], F32)

    lbp = lbp_ref[...]
    e = jnp.exp(lbp - jnp.max(lbp, axis=0, keepdims=True))
    sm = e / jnp.sum(e, axis=0, keepdims=True)
    lb = jnp.sum(sm[:layer + 1], axis=0, keepdims=True)

    L = length
    C = min(SUB, L)
    tril = _tri(L, True)
    row_c = lax.broadcasted_iota(jnp.int32, (C, LANES), 0)

    states = [st_scr[bi] for bi in range(bb)]
    for c, bi in [(c, bi) for c in range(n_chunks) for bi in range(bb)]:
        rows = slice(c * L, (c + 1) * L)
        f = lb + (1.0 - lb) * _sigmoid(f_ref[bi, rows, :])
        logf = jnp.log(f)
        kk = 1.0 - f
        if t_valid is not None:
            valid = lax.broadcasted_iota(jnp.int32, (L, LANES), 0) < t_valid
            logf = jnp.where(valid, logf, 0.0)
            kk = jnp.where(valid, kk, 0.0)
        q = q_ref[bi, rows, :] * (LANES ** -0.5)
        iv = i_ref[bi, rows, :]
        bcum = _cumsum_rows(logf)
        st = states[bi]
        o_inter = _dot_nt((q * jnp.exp(bcum)).astype(BF16), st.astype(BF16))
        for sc in range(L // C):
            lo = sc * C
            q_s, b_s = q[lo:lo + C], bcum[lo:lo + C]
            o_s = o_inter[lo:lo + C]
            if sc > 0:
                ref_row = bcum[lo - 1:lo]
                qt = q_s * jnp.exp(b_s - ref_row)
                kt = kk[:lo] * jnp.exp(ref_row - bcum[:lo])
                a = _dot_nt(qt.astype(BF16), kt.astype(BF16))
                o_s = o_s + _dot(a.astype(BF16), iv[:lo].astype(BF16))
            b2_s = b_s * LOG2E
            for s in range(C):
                dec = jnp.where(row_c >= s, jnp.exp2(b2_s - b2_s[s:s + 1]), 0.0)
                a_col = jnp.sum(q_s * kk[lo + s:lo + s + 1] * dec, axis=-1, keepdims=True)
                o_s = o_s + a_col * iv[lo + s:lo + s + 1]
            gate = g_ref[bi, c * L + lo:c * L + lo + C, :]
            o_ref[bi, c * L + lo:c * L + lo + C, :] = (
                _rms(o_s, gn_ref[...]) * (gate * _sigmoid(gate))).astype(o_ref.dtype)
        b_last = bcum[L - 1:L]
        kd = kk * jnp.exp(b_last - bcum)
        states[bi] = st * jnp.exp(b_last) + _dot_tn(iv.astype(BF16), kd.astype(BF16))
    for bi in range(bb):
        st_scr[bi] = states[bi]

    @pl.when(tb == pl.num_programs(2) - 1)
    def _():
        for bi in range(bb):
            s_ref[bi, 0] = st_scr[bi].T


def _hgrn2(q, f, i, g, lb_param, gnorm, s0, *, layer, length, n_chunks, bb=1, t_valid=None):
    b, tt, w = q.shape
    nh = w // LANES
    tb = length * n_chunks
    assert b % bb == 0 and tt % tb == 0, (b, bb, tt, tb)
    xspec = pl.BlockSpec((bb, tb, LANES), lambda bi, h, ti: (bi, ti, h))
    sspec = pl.BlockSpec((bb, 1, LANES, LANES), lambda bi, h, ti: (bi, h, 0, 0))
    nl = lb_param.shape[0]
    in_specs = [xspec, xspec, xspec, xspec,
                pl.BlockSpec((nl, LANES), lambda bi, h, ti: (0, h)),
                pl.BlockSpec((1, LANES), lambda bi, h, ti: (0, 0))]
    args = [q, f, i, g, lb_param, gnorm.reshape(1, LANES)]
    if s0 is not None:
        in_specs.append(sspec)
        args.append(s0)
    return pl.pallas_call(
        functools.partial(_hgrn2_kernel, layer=layer, n_layers=nl - 1, length=length,
                          n_chunks=n_chunks, bb=bb, t_valid=t_valid, has_state=s0 is not None),
        grid=(b // bb, nh, tt // tb),
        in_specs=in_specs,
        out_specs=[xspec, sspec],
        out_shape=[jax.ShapeDtypeStruct((b, tt, w), BF16),
                   jax.ShapeDtypeStruct((b, nh, LANES, LANES), F32)],
        scratch_shapes=[pltpu.VMEM((bb, LANES, LANES), F32)],
        compiler_params=_params("parallel", "parallel", "arbitrary"),
        name="hgrn2",
    )(*args)


def _mlstm_kernel(*refs, length, n_chunks, bb, t_valid, has_state, i_lane0, f_lane0):
    if has_state:
        (q_ref, k_ref, v_ref, og_ref, gt_ref, gb_ref, gn_ref, c0_ref, n0_ref, m0_ref,
         o_ref, c_ref, n_ref, m_ref, c_scr, n_scr, m_scr) = refs
    else:
        (q_ref, k_ref, v_ref, og_ref, gt_ref, gb_ref, gn_ref,
         o_ref, c_ref, n_ref, m_ref, c_scr, n_scr, m_scr) = refs
    hp, tb = pl.program_id(1), pl.program_id(2)

    @pl.when(tb == 0)
    def _():
        top = lax.broadcasted_iota(jnp.int32, (LANES, LANES), 0) < HALF
        for bi in range(bb):
            if has_state:
                c0 = c0_ref[bi, 0]
                c_scr[bi] = jnp.concatenate([jnp.where(top, c0, 0.0), jnp.where(top, 0.0, c0)], axis=1)
                n_scr[bi] = n0_ref[bi, 0]
                m_scr[bi] = m0_ref[bi, 0]
            else:
                c_scr[bi] = jnp.zeros(c_scr.shape[1:], F32)
                n_scr[bi] = jnp.zeros(n_scr.shape[1:], F32)
                m_scr[bi] = jnp.zeros(m_scr.shape[1:], F32)

    L = length
    lane = lax.broadcasted_iota(jnp.int32, (L, LANES), 1)
    lane2 = lax.broadcasted_iota(jnp.int32, (L, 2 * L), 1)
    row2 = lax.broadcasted_iota(jnp.int32, (L, 2 * L), 0)
    left2 = lane2 < L
    causal2 = jnp.where(left2, lane2, lane2 - L) <= row2
    left_v = lax.broadcasted_iota(jnp.int32, (L, 2 * LANES), 1) < LANES
    left_q = lane < HALF
    nlane = lax.broadcasted_iota(jnp.int32, (1, LANES), 1)
    crow = lax.broadcasted_iota(jnp.int32, (LANES, 2 * LANES), 0)
    ccol = lax.broadcasted_iota(jnp.int32, (LANES, 2 * LANES), 1)
    own_block = (crow < HALF) == (ccol < LANES)
    is_f = (lane >= f_lane0) & (lane < f_lane0 + 4)
    two = lambda cond, a: jnp.where(cond, a[0], a[1])
    cs = [c_scr[bi] for bi in range(bb)]
    ns = [n_scr[bi] for bi in range(bb)]
    ms = [m_scr[bi] for bi in range(bb)]
    for c, bi in [(c, bi) for c in range(n_chunks) for bi in range(bb)]:
        rows = slice(c * L, (c + 1) * L)
        gpre = gt_ref[bi, rows, :] + gb_ref[...]
        x = jnp.where(is_f, _log_sigmoid(gpre), gpre)
        if t_valid is not None:
            valid = lax.broadcasted_iota(jnp.int32, (L, LANES), 0) < t_valid
            x = jnp.where(valid, x, jnp.where(is_f, 0.0, NEG))
        bcol_all = _cumsum_rows(x)
        xt = x.T
        brow_all = bcol_all.T
        srow = lax.broadcasted_iota(jnp.int32, xt.shape, 0)
        pick = lambda a, l: jnp.sum(jnp.where(lane == l, a, 0.0), axis=-1, keepdims=True)
        pick_t = lambda a, l: jnp.sum(jnp.where(srow == l, a, 0.0), axis=0, keepdims=True)
        heads = (2 * hp, 2 * hp + 1)
        b_col = [pick(bcol_all, f_lane0 + h) for h in heads]
        ig_col = [pick(x, i_lane0 + h) for h in heads]
        b_row2 = jnp.concatenate([pick_t(brow_all, f_lane0 + h) for h in heads], axis=1)
        ig_row2 = jnp.concatenate([pick_t(xt, i_lane0 + h) for h in heads], axis=1)
        m_old = [jnp.sum(jnp.where(nlane == j, ms[bi], 0.0), axis=-1, keepdims=True) for j in range(2)]
        d2 = jnp.where(causal2, two(left2, b_col) - b_row2 + ig_row2, NEG)
        inter = [b_col[j] + m_old[j] for j in range(2)]
        m_t = [jnp.maximum(inter[0], jnp.max(jnp.where(left2, d2, NEG), axis=-1, keepdims=True)),
               jnp.maximum(inter[1], jnp.max(jnp.where(left2, NEG, d2), axis=-1, keepdims=True))]
        w_inter = [jnp.exp(inter[j] - m_t[j]) for j in range(2)]
        wmat2 = jnp.exp(d2 - two(left2, m_t))

        q = q_ref[bi, rows, :]
        k = k_ref[bi, rows, :] * (HALF ** -0.5)
        v = v_ref[bi, rows, :]
        qb = q.astype(BF16)
        cst, nst = cs[bi], ns[bi]
        k_bd = jnp.concatenate([jnp.where(left_q, k, 0.0), jnp.where(left_q, 0.0, k)], axis=0)
        v_bd = jnp.concatenate([jnp.where(left_v, v, 0.0), jnp.where(left_v, 0.0, v)], axis=0)
        qk2 = _dot_nt(qb, k_bd.astype(BF16)) * wmat2
        num2 = (two(left_v, w_inter) * _dot(qb, cst.astype(BF16))
                + _dot(qk2.astype(BF16), v_bd.astype(BF16)))
        qn = q * nst
        den = [w_inter[0] * jnp.sum(jnp.where(left_q, qn, 0.0), axis=-1, keepdims=True)
               + jnp.sum(jnp.where(left2, qk2, 0.0), axis=-1, keepdims=True),
               w_inter[1] * jnp.sum(jnp.where(left_q, 0.0, qn), axis=-1, keepdims=True)
               + jnp.sum(jnp.where(left2, 0.0, qk2), axis=-1, keepdims=True)]
        floor = [jnp.maximum(jnp.abs(den[j]), jnp.exp(-m_t[j])) for j in range(2)]
        hd2 = num2 / two(left_v, floor)
        og = og_ref[bi, rows, :]
        for j in range(2):
            cols = slice(j * LANES, (j + 1) * LANES)
            o_ref[bi, rows, cols] = (_rms(hd2[:, cols], gn_ref[...]) * _sigmoid(og[:, cols])).astype(o_ref.dtype)

        m_new = [m_t[j][L - 1:L] for j in range(2)]
        b_last = [b_col[j][L - 1:L] for j in range(2)]
        w_c = [jnp.exp(b_last[j] + m_old[j] - m_new[j]) for j in range(2)]
        w_s = [jnp.exp(b_last[j] - b_col[j] + ig_col[j] - m_new[j]) for j in range(2)]
        ks = k * two(left_q, w_s)
        upd = _dot_tn(ks.astype(BF16), v.astype(BF16))
        cs[bi] = two(crow < HALF, w_c) * cst + jnp.where(own_block, upd, 0.0)
        ns[bi] = two(nlane < HALF, w_c) * nst + jnp.sum(ks, axis=0, keepdims=True)
        ms[bi] = jnp.where(nlane == 0, m_new[0], jnp.where(nlane == 1, m_new[1], ms[bi]))
    for bi in range(bb):
        c_scr[bi], n_scr[bi], m_scr[bi] = cs[bi], ns[bi], ms[bi]

    @pl.when(tb == pl.num_programs(2) - 1)
    def _():
        for bi in range(bb):
            c_ref[bi, 0] = c_scr[bi, :, :LANES] + c_scr[bi, :, LANES:]
            n_ref[bi, 0] = n_scr[bi]
            m_ref[bi, 0] = m_scr[bi]


def _mlstm(qk, v, og, gates, gate_bias, gnorm, state, *, length, n_chunks, i_lane0, f_lane0, bb=1,
           t_valid=None):
    b, tt, _ = qk.shape
    tb = length * n_chunks
    assert b % bb == 0 and tt % tb == 0, (b, bb, tt, tb)
    npair = 2
    qspec = pl.BlockSpec((bb, tb, LANES), lambda bi, hp, ti: (bi, ti, hp))
    kspec = pl.BlockSpec((bb, tb, LANES), lambda bi, hp, ti: (bi, ti, npair + hp))
    vspec = pl.BlockSpec((bb, tb, 2 * LANES), lambda bi, hp, ti: (bi, ti, hp))
    gspec = pl.BlockSpec((bb, tb, LANES), lambda bi, hp, ti: (bi, ti, 0))
    row = pl.BlockSpec((1, LANES), lambda bi, hp, ti: (0, 0))
    cspec = pl.BlockSpec((bb, 1, LANES, LANES), lambda bi, hp, ti: (bi, hp, 0, 0))
    nspec = pl.BlockSpec((bb, 1, 1, LANES), lambda bi, hp, ti: (bi, hp, 0, 0))
    in_specs = [qspec, kspec, vspec, vspec, gspec, row, row]
    args = [qk, qk, v, og, gates, gate_bias, gnorm.reshape(1, LANES)]
    if state is not None:
        in_specs += [cspec, nspec, nspec]
        args += list(state)
    return pl.pallas_call(
        functools.partial(_mlstm_kernel, length=length, n_chunks=n_chunks, bb=bb, t_valid=t_valid,
                          has_state=state is not None, i_lane0=i_lane0, f_lane0=f_lane0),
        grid=(b // bb, npair, tt // tb),
        in_specs=in_specs,
        out_specs=[vspec, cspec, nspec, nspec],
        out_shape=[jax.ShapeDtypeStruct((b, tt, 4 * LANES), BF16),
                   jax.ShapeDtypeStruct((b, npair, LANES, LANES), F32),
                   jax.ShapeDtypeStruct((b, npair, 1, LANES), F32),
                   jax.ShapeDtypeStruct((b, npair, 1, LANES), F32)],
        scratch_shapes=[pltpu.VMEM((bb, LANES, 2 * LANES), F32), pltpu.VMEM((bb, 1, LANES), F32),
                        pltpu.VMEM((bb, 1, LANES), F32)],
        compiler_params=_params("parallel", "parallel", "arbitrary"),
        name="mlstm",
    )(*args)


ROWS = 16


def _pad_rows(x, rows):
    return jnp.concatenate([x, jnp.zeros((rows - x.shape[0], x.shape[1]), x.dtype)], axis=0)


def _decode_diff_kernel(pt_ref, q_ref, kn_ref, vn_ref, tab_ref, lq1, lk1, lq2, lk2, subln_ref, *rest,
                        n_pages, page, n_heads, lam_init):
    del pt_ref
    k_refs, v_refs = rest[:n_pages], rest[n_pages:2 * n_pages]
    o_ref, bias_scr = rest[2 * n_pages], rest[2 * n_pages + 1]
    past = n_pages * page
    pw = page * n_heads
    rr = lax.broadcasted_iota(jnp.int32, (ROWS, pw), 0) % 8
    ll = lax.broadcasted_iota(jnp.int32, (ROWS, pw), 1)
    own = (ll % n_heads == rr) & (rr < n_heads)
    rcol = lax.broadcasted_iota(jnp.int32, (ROWS, 1), 0) % 8

    @pl.when(pl.program_id(0) == 0)
    def _():
        for p in range(n_pages):
            n = past - (p * page + ll // n_heads)
            acc = jnp.zeros((ROWS, pw), F32)
            for h in range(n_heads):
                acc = jnp.where(rr == h, _t5_bias_value(n, tab_ref, h), acc)
            bias_scr[:, p * pw:(p + 1) * pw] = acc

    lane = lax.broadcasted_iota(jnp.int32, (8, LANES), 1)
    q8 = _pad_rows(q_ref[0], 8) * (HALF ** -0.5)
    qm = jnp.concatenate([jnp.where(lane < HALF, q8, 0.0), jnp.where(lane >= HALF, q8, 0.0)], axis=0)
    qb = qm.astype(BF16)
    kn8 = _pad_rows(kn_ref[0], 8)
    bias_new = jnp.zeros((ROWS, 1), F32)
    for h in range(n_heads):
        bias_new = jnp.where(rcol == h, tab_ref[0, h], bias_new)
    s_new = jnp.sum(qm * jnp.concatenate([kn8, kn8], axis=0), axis=-1, keepdims=True) + bias_new
    s_pages = [jnp.where(own, _dot_nt(qb, k_refs[p][...].astype(BF16)) + bias_scr[:, p * pw:(p + 1) * pw], NEG)
               for p in range(n_pages)]
    m = s_new
    for s in s_pages:
        m = jnp.maximum(m, jnp.max(s, axis=-1, keepdims=True))
    e_new = jnp.exp(s_new - m)
    e_pages = [jnp.exp(s - m) for s in s_pages]
    l = e_new
    for e in e_pages:
        l = l + jnp.sum(e, axis=-1, keepdims=True)
    lam = _lam_value(lq1, lk1, lq2, lk2, lam_init)
    inv = 1.0 / l
    comb = lambda a: a[0:8] - lam * a[8:16]
    out = comb(e_new * inv) * _pad_rows(vn_ref[0], 8)
    for p in range(n_pages):
        wgt = _pad_rows(comb(e_pages[p] * inv), ROWS).astype(BF16)
        out = out + _dot(wgt, v_refs[p][...].astype(BF16))[0:8]
    y = _rms(out, subln_ref[...]) * (1.0 - lam_init)
    o_ref[0] = y[0:n_heads].astype(o_ref.dtype)


def _decode_diff(page_table, q, k_new, v_new, k_pool, v_pool, rel_bias, lq1, lk1, lq2, lk2, subln, *, lam_init):
    ns, n_pages = page_table.shape
    n_heads = q.shape[1]
    pw = k_pool.shape[1]
    row = pl.BlockSpec((1, n_heads, LANES), lambda b, pt: (b, 0, 0))
    small = lambda shape: pl.BlockSpec(shape, lambda b, pt: (0,) * len(shape))
    page_spec = lambda p: pl.BlockSpec((None, pw, LANES), lambda b, pt: (pt[b, p], 0, 0))
    in_specs = [row, row, row, pl.BlockSpec(memory_space=pltpu.SMEM),
                small((1, HALF)), small((1, HALF)), small((1, HALF)), small((1, HALF)), small((1, LANES))]
    in_specs += [page_spec(p) for p in range(n_pages)] * 2
    grid_spec = pltpu.PrefetchScalarGridSpec(
        num_scalar_prefetch=1, grid=(ns,), in_specs=in_specs, out_specs=row,
        scratch_shapes=[pltpu.VMEM((ROWS, n_pages * pw), F32)])
    return pl.pallas_call(
        functools.partial(_decode_diff_kernel, n_pages=n_pages, page=pw // n_heads, n_heads=n_heads,
                          lam_init=lam_init),
        grid_spec=grid_spec,
        out_shape=jax.ShapeDtypeStruct((ns, n_heads, LANES), BF16),
        compiler_params=_params("arbitrary"),
        name="decode_diff",
    )(page_table, q, k_new, v_new, rel_bias, lq1, lk1, lq2, lk2, subln,
      *([k_pool] * n_pages), *([v_pool] * n_pages))


def _decode_fox_kernel(pt_ref, q_ref, kn_ref, vn_ref, g_ref, gb_ref, *rest, n_pages, page, n_heads):
    del pt_ref
    f_refs, k_refs, v_refs = rest[:n_pages], rest[n_pages:2 * n_pages], rest[2 * n_pages:3 * n_pages]
    o_ref, lf_ref = rest[3 * n_pages], rest[3 * n_pages + 1]
    w = n_heads * HALF
    r_i = lax.broadcasted_iota(jnp.int32, (ROWS, w), 0)
    lane = lax.broadcasted_iota(jnp.int32, (ROWS, w), 1)
    q = q_ref[0] * (HALF ** -0.5)
    qbig = jnp.where(lane // HALF == r_i, q, 0.0)
    qb = qbig.astype(BF16)
    lf_row = _log_sigmoid(g_ref[0] + gb_ref[...])
    lf_ref[0] = lf_row[:, :n_heads]
    gr = lax.broadcasted_iota(jnp.int32, (ROWS, LANES), 0)
    gl = lax.broadcasted_iota(jnp.int32, (ROWS, LANES), 1)
    lf_col = jnp.sum(jnp.where((gl == gr) & (gr < n_heads), lf_row, 0.0), axis=-1, keepdims=True)
    triu = _tri(page, False)
    carry = jnp.zeros((ROWS, 1), F32)
    prefix = []
    for p in range(n_pages):
        pp = _exact_dot_r(_pad_rows(f_refs[p][...], ROWS), triu) + carry
        prefix.append(pp)
        carry = pp[:, page - 1:page]
    s_new = jnp.sum(qbig * kn_ref[0], axis=-1, keepdims=True)
    s_pages = [_dot(qb, k_refs[p][...].astype(BF16)) + (lf_col + carry - prefix[p])
               for p in range(n_pages)]
    m = s_new
    for s in s_pages:
        m = jnp.maximum(m, jnp.max(s, axis=-1, keepdims=True))
    e_new = jnp.exp(s_new - m)
    e_pages = [jnp.exp(s - m) for s in s_pages]
    l = e_new
    for e in e_pages:
        l = l + jnp.sum(e, axis=-1, keepdims=True)
    inv = 1.0 / l
    out = (e_new * inv) * vn_ref[0]
    for p in range(n_pages):
        out = out + _dot_nt((e_pages[p] * inv).astype(BF16), v_refs[p][...].astype(BF16))
    o_ref[0] = jnp.sum(jnp.where(lane // HALF == r_i, out, 0.0), axis=0, keepdims=True).astype(o_ref.dtype)


def _decode_fox(page_table, q, k_new, v_new, gates, gate_bias, lf_pool_t, k_pool_t, v_pool_t):
    ns, n_pages = page_table.shape
    _, w, page = k_pool_t.shape
    n_heads = w // HALF
    row = pl.BlockSpec((1, 1, w), lambda b, pt: (b, 0, 0))
    grow = pl.BlockSpec((1, 1, LANES), lambda b, pt: (b, 0, 0))
    page_spec = lambda p: pl.BlockSpec((None, w, page), lambda b, pt: (pt[b, p], 0, 0))
    f_spec = lambda p: pl.BlockSpec((None, n_heads, page), lambda b, pt: (pt[b, p], 0, 0))
    in_specs = [row, row, row, grow, pl.BlockSpec((1, LANES), lambda b, pt: (0, 0))]
    in_specs += [f_spec(p) for p in range(n_pages)] + [page_spec(p) for p in range(n_pages)] * 2
    grid_spec = pltpu.PrefetchScalarGridSpec(
        num_scalar_prefetch=1, grid=(ns,), in_specs=in_specs,
        out_specs=[row, pl.BlockSpec((1, 1, n_heads), lambda b, pt: (b, 0, 0))])
    return pl.pallas_call(
        functools.partial(_decode_fox_kernel, n_pages=n_pages, page=page, n_heads=n_heads),
        grid_spec=grid_spec,
        out_shape=[jax.ShapeDtypeStruct((ns, 1, w), BF16), jax.ShapeDtypeStruct((ns, 1, n_heads), F32)],
        compiler_params=_params("arbitrary"),
        name="decode_fox",
    )(page_table, q, k_new, v_new, gates, gate_bias,
      *([lf_pool_t] * n_pages), *([k_pool_t] * n_pages), *([v_pool_t] * n_pages))


def kernel(x_prompt, x_sample, cache_k_a, cache_v_a, state_s_b, cache_k_c, cache_v_c, cache_logf_c,
           state_c_d, state_n_d, state_m_d, page_table, norm_mix, w_in_even, lambda_q1, lambda_k1,
           lambda_q2, lambda_k2, subln_a, rel_bias, lb_param, gnorm_b, w_out_even, w_in_odd, b_f_c,
           b_i_d, b_f_d, gnorm_d, w_out_odd, norm_mlp, w_up, w_down, norm_final):
    B, T, D = x_prompt.shape
    S = x_sample.shape[0]
    n_pool, page = cache_k_a.shape[1], cache_k_a.shape[2]
    h_a, h_c, h_d = cache_k_a.shape[3], cache_k_c.shape[3], state_c_d.shape[2]
    w_a = h_a * LANES
    lam_init = 0.8 - 0.6 * math.exp(-0.3 * 0)
    tq = 512

    w_in0 = w_in_even[0].astype(BF16)
    wo = w_in_odd[0]
    c = [0]
    for sz in (h_c * HALF, h_c * HALF, h_c * HALF, h_c, h_d * HALF, h_d * HALF, h_d * LANES, h_d, h_d,
               h_d * LANES):
        c.append(c[-1] + sz)
    col = lambda i: wo[:, c[i]:c[i + 1]]
    gate_cols = jnp.concatenate([col(3), col(7), col(8)], axis=1)
    gate_cols = jnp.pad(gate_cols, ((0, 0), (0, w_a - gate_cols.shape[1])))
    w_in1 = jnp.concatenate([col(0), col(1), col(2), col(4), col(5), col(6), col(9), gate_cols],
                            axis=1).astype(BF16)
    gate_bias = jnp.pad(jnp.concatenate([b_f_c[0], b_i_d[0], b_f_d[0]]), (0, LANES - h_c - 2 * h_d))
    gate_bias = gate_bias.reshape(1, LANES)
    i_lane0, f_lane0 = h_c, h_c + h_d
    w_out0, w_out1 = w_out_even[0].astype(BF16), w_out_odd[0].astype(BF16)
    w_up_b, w_down_b = w_up.astype(BF16), w_down.astype(BF16)
    lq1, lk1, lq2, lk2 = (a.reshape(1, HALF) for a in (lambda_q1[0], lambda_k1[0], lambda_q2[0], lambda_k2[0]))
    subln = subln_a[0].reshape(1, LANES)

    M = B * T
    xp = x_prompt.reshape(M, D)
    qa, ka, va, qb, fb, ib, gb, ka_hm, va_hm = _norm_matmul(
        xp, norm_mix[0], w_in0, tm=512, tn=w_a, dtypes=[F32, BF16, F32, F32, F32, F32, F32], head_major=(1, 2))
    r3 = lambda a: a.reshape(B, T, w_a)
    bias = _bias_tiles(rel_bias, t=tq)
    o_a = _flash2("diff", r3(qa), r3(ka), r3(va), (bias, rel_bias, lq1, lk1, lq2, lk2, subln),
                  t=tq, lam_init=lam_init)
    o_b, p_s_b = _hgrn2(r3(qb), r3(fb), r3(ib), r3(gb), lb_param, gnorm_b[0], None,
                        layer=0, length=CHUNK, n_chunks=8, bb=2)
    x2 = _mix_mlp(o_a.reshape(M, w_a), o_b.reshape(M, w_a), w_out0, xp,
                  norm_mlp[0], w_up_b[0], w_down_b[0], tm=512)
    w_kv_t = jnp.concatenate([col(1), col(2)], axis=1).T.astype(BF16)
    qc, kc, qkd, vd, od, gts, kc_t, vc_t = _norm_matmul(
        x2, norm_mix[1], w_in1, tm=512, tn=w_a, dtypes=[F32, BF16, None, F32, F32, F32, F32], wt=w_kv_t, seq=T)
    p_lf_c, fcol, frow = _fox_gates(r3(gts), gate_bias, tc=512, nh=h_c)
    o_c = _flash2("fox", r3(qc), r3(kc), vc_t, (fcol, frow), t=tq, v_transposed=True)
    o_d, p_c, p_n, p_m = _mlstm(r3(qkd), r3(vd), r3(od), r3(gts), gate_bias, gnorm_d[0], None,
                                length=CHUNK, n_chunks=4, bb=2, i_lane0=i_lane0, f_lane0=f_lane0)
    y_prompt = _mix_mlp(o_c.reshape(M, w_a), o_d.reshape(M, w_a), w_out1, x2,
                        norm_mlp[1], w_up_b[1], w_down_b[1], tm=512, final_g=norm_final).reshape(B, T, D)

    PADT = 8
    DEC_BB = 16
    xs = x_sample.reshape(S, D)
    sqa, ska, sva, sqb, sfb, sib, sgb = _norm_matmul(xs, norm_mix[0], w_in0, tm=S, tn=w_a)
    s3 = lambda a: a.reshape(S, 1, w_a)
    padt = lambda a: jnp.pad(a.reshape(S, 1, w_a), ((0, 0), (0, PADT - 1), (0, 0)))
    sh = lambda a: a.reshape(S, h_a, LANES)
    so_a = _decode_diff(page_table, sh(sqa), sh(ska), sh(sva),
                        cache_k_a.reshape(n_pool, page * h_a, LANES), cache_v_a.reshape(n_pool, page * h_a, LANES),
                        rel_bias, lq1, lk1, lq2, lk2, subln, lam_init=lam_init)
    so_b, s_s_b = _hgrn2(padt(sqb), padt(sfb), padt(sib), padt(sgb), lb_param, gnorm_b[0], state_s_b[0],
                         layer=0, length=PADT, n_chunks=1, bb=DEC_BB, t_valid=1)
    sx2 = _mix_mlp(so_a.reshape(S, w_a), so_b[:, 0], w_out0, xs, norm_mlp[0], w_up_b[0], w_down_b[0], tm=S)
    sqc, skc, svc, sqkd, svd, sod, sgts = _norm_matmul(sx2, norm_mix[1], w_in1, tm=S, tn=w_a)
    lf_pool_t = jnp.swapaxes(cache_logf_c[0], 1, 2)
    pool_t = lambda a: jnp.transpose(a[0], (0, 2, 3, 1)).reshape(n_pool, w_a, page)
    so_c, s_lf_c = _decode_fox(page_table, s3(sqc), s3(skc), s3(svc), sgts[:, :LANES].reshape(S, 1, LANES),
                               gate_bias, lf_pool_t, pool_t(cache_k_c), pool_t(cache_v_c))
    npair = h_d // 2
    m0 = jnp.pad(state_m_d[0].reshape(S, npair, 1, 2), ((0, 0), (0, 0), (0, 0), (0, LANES - 2)))
    so_d, s_c, s_n, s_m = _mlstm(padt(sqkd), padt(svd), padt(sod), padt(sgts), gate_bias, gnorm_d[0],
                                 (state_c_d[0].reshape(S, npair, LANES, LANES),
                                  state_n_d[0].reshape(S, npair, 1, LANES), m0),
                                 length=PADT, n_chunks=1, i_lane0=i_lane0, f_lane0=f_lane0, bb=DEC_BB,
                                 t_valid=1)
    y_sample = _mix_mlp(so_c.reshape(S, w_a), so_d[:, 0], w_out1, sx2, norm_mlp[1], w_up_b[1], w_down_b[1],
                        tm=S, final_g=norm_final).reshape(S, 1, D)

    dk_d = state_c_d.shape[3]
    heads = lambda a, nb, tt, nh: a.reshape(1, nb, tt, nh, w_a // nh)
    unpair_c = lambda a, nb: a.reshape(1, nb, h_d, dk_d, LANES)
    unpair_n = lambda a, nb: a.reshape(1, nb, h_d, dk_d)
    unpair_m = lambda a, nb: a[..., :2].reshape(1, nb, h_d)
    from_t = lambda a: jnp.transpose(a.reshape(1, B, h_c, HALF, T), (0, 1, 4, 2, 3))
    return (y_prompt, y_sample,
            ka_hm.reshape(1, B, T, h_a, LANES), va_hm.reshape(1, B, T, h_a, LANES), p_s_b[None],
            from_t(kc_t), from_t(vc_t), p_lf_c[None],
            unpair_c(p_c, B), unpair_n(p_n, B), unpair_m(p_m, B),
            heads(ska, S, 1, h_a), heads(sva, S, 1, h_a), s_s_b[None],
            heads(skc, S, 1, h_c), heads(svc, S, 1, h_c), s_lf_c[None],
            unpair_c(s_c, S), unpair_n(s_n, S), unpair_m(s_m, S))
```

```python
import functools
import math

import jax
import jax.numpy as jnp
from jax import lax
from jax.experimental import pallas as pl
from jax.experimental.pallas import tpu as pltpu

F32 = jnp.float32
BF16 = jnp.bfloat16
EPS = 1e-6
NEG = -1e30
LOG2E = math.log2(math.e)
ZERO_PROB_LOG = 110.0

LANES = 128
HALF = 64
NUM_BUCKETS = 32
MAX_EXACT = NUM_BUCKETS // 2
MAX_DISTANCE = 128
CHUNK = 64
SUB = 16
VMEM_LIMIT_BYTES = 56 * 1024 * 1024


def _params(*sem):
    return pltpu.CompilerParams(dimension_semantics=sem, vmem_limit_bytes=VMEM_LIMIT_BYTES)


def _dot(a, b):
    return jnp.dot(a, b, preferred_element_type=F32)


def _dot_nt(a, b):
    return lax.dot_general(a, b, (((1,), (1,)), ((), ())), preferred_element_type=F32)


def _dot_tn(a, b):
    return lax.dot_general(a, b, (((0,), (0,)), ((), ())), preferred_element_type=F32)


def _split3(x):
    hi = x.astype(BF16)
    r = x - hi.astype(F32)
    mid = r.astype(BF16)
    lo = (r - mid.astype(F32)).astype(BF16)
    return hi, mid, lo


def _exact_dot(onesmat, x):
    hi, mid, lo = _split3(x)
    return _dot(onesmat, hi) + _dot(onesmat, mid) + _dot(onesmat, lo)


def _exact_dot_tn(x, onesmat):
    hi, mid, lo = _split3(x)
    return _dot_tn(hi, onesmat) + _dot_tn(mid, onesmat) + _dot_tn(lo, onesmat)


def _exact_dot_r(x, onesmat):
    hi, mid, lo = _split3(x)
    return _dot(hi, onesmat) + _dot(mid, onesmat) + _dot(lo, onesmat)


def _cumsum_rows(x):
    n = x.shape[0]
    row = lax.broadcasted_iota(jnp.int32, x.shape, 0)
    shift = 1
    while shift < n:
        x = x + jnp.where(row >= shift, pltpu.roll(x, shift, 0), 0.0)
        shift *= 2
    return x


def _tri(n, lower):
    r = lax.broadcasted_iota(jnp.int32, (n, n), 0)
    c = lax.broadcasted_iota(jnp.int32, (n, n), 1)
    keep = (c <= r) if lower else (r <= c)
    return jnp.where(keep, 1.0, 0.0).astype(BF16)


def _eye(n):
    r = lax.broadcasted_iota(jnp.int32, (n, n), 0)
    c = lax.broadcasted_iota(jnp.int32, (n, n), 1)
    return jnp.where(r == c, 1.0, 0.0).astype(BF16)


def _log_sigmoid(x):
    return jnp.minimum(x, 0.0) - jnp.log1p(jnp.exp(-jnp.abs(x)))


def _sigmoid(x):
    return 1.0 / (1.0 + jnp.exp(-x))


def _rms(x, g):
    return x * lax.rsqrt(jnp.mean(x * x, axis=-1, keepdims=True) + EPS) * g


def _norm_mm_kernel(x_ref, g_ref, w_ref, *rest, tn, tiles, n_t, head_major):
    h = _rms(x_ref[...], g_ref[...]).astype(BF16)
    if n_t:
        wt_ref, rest = rest[0], rest[1:]
    tm = x_ref.shape[0]
    nh = tn // LANES
    extra = iter(rest[len(tiles):])
    for o_ref, j in zip(rest, tiles):
        z = _dot(h, w_ref[:, j * tn:(j + 1) * tn])
        o_ref[...] = z.astype(o_ref.dtype)
        if j in head_major:
            hm_ref = next(extra)
            for hd in range(nh):
                hm_ref[pl.ds(hd, tm, stride=nh), :] = z[:, hd * LANES:(hd + 1) * LANES]
    for j in range(n_t):
        next(extra)[0] = _dot_nt(wt_ref[j * tn:(j + 1) * tn, :], h)


def _norm_matmul(x, g, w, *, tm, tn, dtypes=None, head_major=(), wt=None, seq=None):
    m, d = x.shape
    n = w.shape[1]
    dtypes = dtypes or [F32] * (n // tn)
    tiles = [j for j, dt in enumerate(dtypes) if dt is not None]
    nh = tn // LANES
    in_specs = [pl.BlockSpec((tm, d), lambda i: (i, 0)),
                pl.BlockSpec((1, d), lambda i: (0, 0)),
                pl.BlockSpec((d, n), lambda i: (0, 0))]
    args = [x, g.reshape(1, d), w]
    out_specs = [pl.BlockSpec((tm, tn), lambda i: (i, 0))] * len(tiles)
    out_shape = [jax.ShapeDtypeStruct((m, tn), dtypes[j]) for j in tiles]
    out_specs += [pl.BlockSpec((tm * nh, LANES), lambda i: (i, 0))] * len(head_major)
    out_shape += [jax.ShapeDtypeStruct((m * nh, LANES), F32)] * len(head_major)
    n_t = 0
    if wt is not None:
        n_t = wt.shape[0] // tn
        per_seq = seq // tm
        in_specs.append(pl.BlockSpec(wt.shape, lambda i: (0, 0)))
        args.append(wt)
        out_specs += [pl.BlockSpec((1, tn, tm), lambda i: (i // per_seq, 0, i % per_seq))] * n_t
        out_shape += [jax.ShapeDtypeStruct((m // seq, tn, seq), F32)] * n_t
    return pl.pallas_call(
        functools.partial(_norm_mm_kernel, tn=tn, tiles=tiles, n_t=n_t, head_major=tuple(head_major)),
        grid=(m // tm,),
        in_specs=in_specs,
        out_specs=out_specs,
        out_shape=out_shape,
        compiler_params=_params("parallel"),
        name="norm_matmul",
    )(*args)


def _mix_mlp_kernel(a1_ref, a2_ref, wo_ref, res_ref, g_ref, wu_ref, wd_ref, *rest, tf, final_norm):
    o_ref = rest[-1]
    x = res_ref[...] + _dot(jnp.concatenate([a1_ref[...], a2_ref[...]], axis=1), wo_ref[...])
    h = _rms(x, g_ref[...]).astype(BF16)
    acc = x
    for c in range(wu_ref.shape[1] // tf):
        u = _dot(h, wu_ref[:, c * tf:(c + 1) * tf])
        u = jnp.square(jnp.maximum(u, 0.0)).astype(BF16)
        acc = acc + _dot(u, wd_ref[c * tf:(c + 1) * tf, :])
    if final_norm:
        acc = _rms(acc, rest[0][...])
    o_ref[...] = acc


def _mix_mlp(a1, a2, w_out, res, g, w_up, w_down, *, tm, tf=512, final_g=None):
    m, d = res.shape
    f = w_up.shape[1]
    whole = lambda a: pl.BlockSpec(a.shape, lambda i: (0, 0))
    rows = lambda a: pl.BlockSpec((tm, a.shape[1]), lambda i: (i, 0))
    g2 = g.reshape(1, d)
    in_specs = [rows(a1), rows(a2), whole(w_out), rows(res), whole(g2), whole(w_up), whole(w_down)]
    args = [a1, a2, w_out, res, g2, w_up, w_down]
    if final_g is not None:
        args.append(final_g.reshape(1, d))
        in_specs.append(whole(args[-1]))
    return pl.pallas_call(
        functools.partial(_mix_mlp_kernel, tf=tf, final_norm=final_g is not None),
        grid=(m // tm,),
        in_specs=in_specs,
        out_specs=pl.BlockSpec((tm, d), lambda i: (i, 0)),
        out_shape=jax.ShapeDtypeStruct((m, d), F32),
        compiler_params=_params("parallel"),
        name="mix_mlp",
    )(*args)


def _t5_bias_value(n, tab_ref, h):
    val = jnp.zeros(n.shape, F32)
    for b in range(MAX_EXACT):
        val = jnp.where(n == b, tab_ref[b, h], val)
    n_log = NUM_BUCKETS - MAX_EXACT
    for k in range(n_log):
        start = math.ceil(MAX_EXACT * (MAX_DISTANCE / MAX_EXACT) ** (k / n_log))
        val = jnp.where(n >= start, tab_ref[MAX_EXACT + k, h], val)
    return val


def _bias_tiles_kernel(tab_ref, o_ref, *, t):
    h, d = pl.program_id(0), pl.program_id(1)
    i = lax.broadcasted_iota(jnp.int32, (t, t), 0)
    j = lax.broadcasted_iota(jnp.int32, (t, t), 1)
    o_ref[0, 0] = _t5_bias_value(jnp.maximum(d * t + j - i, 0), tab_ref, h) * LOG2E


def _bias_tiles(rel_bias, *, t):
    nh = rel_bias.shape[1]
    return pl.pallas_call(
        functools.partial(_bias_tiles_kernel, t=t),
        grid=(nh, 2),
        in_specs=[pl.BlockSpec(memory_space=pltpu.SMEM)],
        out_specs=pl.BlockSpec((1, 1, t, t), lambda h, d: (h, d, 0, 0)),
        out_shape=jax.ShapeDtypeStruct((nh, 2, t, t), F32),
        compiler_params=_params("arbitrary", "arbitrary"),
        name="t5_bias_tiles",
    )(rel_bias)


def _lam_value(lq1, lk1, lq2, lk2, lam_init):
    s1 = jnp.sum(lq1[...] * lk1[...], axis=-1, keepdims=True)
    s2 = jnp.sum(lq2[...] * lk2[...], axis=-1, keepdims=True)
    return jnp.exp(s1) - jnp.exp(s2) + lam_init


def _flash2_kernel(*refs, mode, t, lam_init, v_transposed):
    if mode == "diff":
        (q_ref, k_ref, v_ref, bias_ref, tab_ref, lq1, lk1, lq2, lk2, subln_ref,
         o_ref, m_scr, l_scr, acc_scr, kb_scr, vt_scr) = refs
    else:
        (q_ref, k_ref, v_ref, fq_ref, fk_ref, fend_ref, o_ref, m_scr, l_scr, acc_scr, kb_scr, vt_scr,
         kn_scr) = refs
    hg, qi = pl.program_id(1), pl.program_id(2)
    n_blk = k_ref.shape[1] // t
    lane = lax.broadcasted_iota(jnp.int32, (t, LANES), 1)

    def head_sq_norm_max(x):
        sq = x * x
        return [jnp.max(jnp.sum(jnp.where(lane < HALF, sq, 0.0), axis=-1, keepdims=True), axis=0, keepdims=True),
                jnp.max(jnp.sum(jnp.where(lane < HALF, 0.0, sq), axis=-1, keepdims=True), axis=0, keepdims=True)]

    @pl.when(qi == 0)
    def _():
        kmax = [jnp.zeros((1, 1), F32), jnp.zeros((1, 1), F32)]
        for kb in range(n_blk):
            kblk = k_ref[0, kb * t:(kb + 1) * t, :]
            kb_scr[kb * t:(kb + 1) * t, :] = kblk.astype(BF16)
            if v_transposed:
                vt_scr[:, kb * t:(kb + 1) * t] = v_ref[0, :, kb * t:(kb + 1) * t].astype(BF16)
            else:
                vt_scr[:, kb * t:(kb + 1) * t] = v_ref[0, kb * t:(kb + 1) * t, :].T.astype(BF16)
            if mode == "fox":
                kmax = [jnp.maximum(a, b) for a, b in zip(kmax, head_sq_norm_max(kblk.astype(F32)))]
        if mode == "fox":
            nl = lax.broadcasted_iota(jnp.int32, (1, LANES), 1)
            kn_scr[...] = jnp.where(nl == 0, kmax[0], jnp.where(nl == 1, kmax[1], 0.0))

    q = q_ref[0] * (HALF ** -0.5 * LOG2E)
    qs = (jnp.where(lane < HALF, q, 0.0).astype(BF16),
          jnp.where(lane >= HALF, q, 0.0).astype(BF16))
    m_scr[...] = jnp.full(m_scr.shape, NEG, F32)
    l_scr[...] = jnp.zeros(l_scr.shape, F32)
    acc_scr[...] = jnp.zeros(acc_scr.shape, F32)
    if mode == "fox":
        fq = fq_ref[0]
        fq_row = lax.broadcasted_iota(jnp.int32, fq.shape, 0)
        fq_raw = [jnp.sum(jnp.where(fq_row == 2 * hg + j, fq, 0.0), axis=0, keepdims=True) for j in range(2)]
        fq_rows = [r * LOG2E for r in fq_raw]

        fend = fend_ref[0]
        fe_row = lax.broadcasted_iota(jnp.int32, fend.shape, 0)
        qmax = head_sq_norm_max(q_ref[0])
        nl = lax.broadcasted_iota(jnp.int32, (1, LANES), 1)
        blk = lax.broadcasted_iota(jnp.int32, (1, n_blk), 1)
        skip = blk < qi
        for j in range(2):
            kmax = jnp.sum(jnp.where(nl == j, kn_scr[...], 0.0), axis=-1, keepdims=True)
            reach = 2.0 * jnp.sqrt(qmax[j] * kmax) * (HALF ** -0.5)
            fend_j = jnp.sum(jnp.where(fe_row == 2 * hg + j, fend, 0.0), axis=0, keepdims=True)
            skip = skip & (reach + fq_raw[j][:, 0:1] - fend_j <= -ZERO_PROB_LOG)
        first_blk = jnp.sum(skip.astype(jnp.int32))

    def step(kb, kind):
        start = pl.multiple_of(kb * t, t)
        k = kb_scr[pl.ds(start, t), :]
        vt = vt_scr[:, pl.ds(start, t)]
        for j in range(2):
            s = _dot_nt(k, qs[j])
            if mode == "diff":
                if kind == "far":
                    shift = tab_ref[NUM_BUCKETS - 1, hg] * LOG2E
                else:
                    shift = 0.0
                    s = s + bias_ref[0, 1]
            else:
                fk = fk_ref[0, pl.ds(start, t), :]
                fk_col = jnp.sum(jnp.where(lax.broadcasted_iota(jnp.int32, fk.shape, 1) == 2 * hg + j, fk, 0.0),
                                 axis=-1, keepdims=True)
                s = s - fk_col * LOG2E
                shift = fq_rows[j]
            m_prev = m_scr[j]
            m_new = jnp.maximum(m_prev, jnp.max(s, axis=0, keepdims=True) + shift)
            alpha = jnp.exp2(m_prev - m_new)
            p = jnp.exp2(s - (m_new - shift))
            l_scr[j] = alpha * l_scr[j] + jnp.sum(p, axis=0, keepdims=True)
            acc_scr[j] = alpha * acc_scr[j] + _dot(vt, p.astype(BF16))
            m_scr[j] = m_new

    if mode == "diff":
        def far_body(kb, carry):
            step(kb, "far")
            return carry
        lax.fori_loop(0, qi - 1, far_body, 0)

        @pl.when(qi >= 1)
        def _():
            step(qi - 1, "sub")
    else:
        def far_body(kb, carry):
            step(kb, "far")
            return carry
        lax.fori_loop(first_blk, qi, far_body, 0)

    def diag_step():
        hq = t // 2
        start = pl.multiple_of(qi * t, t)
        k_halves = (kb_scr[pl.ds(start, hq), :], kb_scr[pl.ds(start + hq, hq), :])
        vt_halves = (vt_scr[:, pl.ds(start, hq)], vt_scr[:, pl.ds(start + hq, hq)])
        if mode == "fox":
            fk = fk_ref[0, pl.ds(start, t), :]
            fk_lane = lax.broadcasted_iota(jnp.int32, fk.shape, 1)
        pad_q = lambda a, fill: jnp.concatenate([jnp.full((a.shape[0], hq), fill, F32), a], axis=1)
        for j in range(2):
            s_top = _dot_nt(k_halves[0], qs[j])
            s_bot = _dot_nt(k_halves[1], qs[j][hq:])
            if mode == "diff":
                shift = jnp.zeros((1, t), F32)
                s_top = s_top + bias_ref[0, 0, :hq, :]
                s_bot = s_bot + bias_ref[0, 0, hq:, hq:]
            else:
                fk_col = jnp.sum(jnp.where(fk_lane == 2 * hg + j, fk, 0.0), axis=-1, keepdims=True) * LOG2E
                shift = fq_rows[j]
                s_top = s_top - fk_col[:hq]
                s_bot = s_bot - fk_col[hq:]
            s_top = jnp.where(lax.broadcasted_iota(jnp.int32, (hq, t), 0)
                              <= lax.broadcasted_iota(jnp.int32, (hq, t), 1), s_top, NEG)
            s_bot = jnp.where(lax.broadcasted_iota(jnp.int32, (hq, hq), 0)
                              <= lax.broadcasted_iota(jnp.int32, (hq, hq), 1), s_bot, NEG)
            m_prev = m_scr[j]
            m_blk = jnp.maximum(jnp.max(s_top, axis=0, keepdims=True),
                                pad_q(jnp.max(s_bot, axis=0, keepdims=True), NEG)) + shift
            m_new = jnp.maximum(m_prev, m_blk)
            alpha = jnp.exp2(m_prev - m_new)
            ref_row = m_new - shift
            p_top = jnp.exp2(s_top - ref_row)
            p_bot = jnp.exp2(s_bot - ref_row[:, hq:])
            l_scr[j] = (alpha * l_scr[j] + jnp.sum(p_top, axis=0, keepdims=True)
                        + pad_q(jnp.sum(p_bot, axis=0, keepdims=True), 0.0))
            acc_scr[j] = (alpha * acc_scr[j] + _dot(vt_halves[0], p_top.astype(BF16))
                          + pad_q(_dot(vt_halves[1], p_bot.astype(BF16)), 0.0))
            m_scr[j] = m_new

    diag_step()

    o0 = acc_scr[0] / l_scr[0]
    o1 = acc_scr[1] / l_scr[1]
    if mode == "diff":
        lam = _lam_value(lq1, lk1, lq2, lk2, lam_init)
        o = (o0 - lam * o1).T
        o_ref[0] = (_rms(o, subln_ref[...]) * (1.0 - lam_init)).astype(o_ref.dtype)
    else:
        row = lax.broadcasted_iota(jnp.int32, (LANES, t), 0)
        o_ref[0] = jnp.where(row < HALF, o0, o1).T.astype(o_ref.dtype)


def _flash2(mode, q, k, v, extra, *, t, lam_init=0.0, v_transposed=False):
    b, tt, w = q.shape
    ng = w // LANES
    qspec = pl.BlockSpec((1, t, LANES), lambda bi, g, qi: (bi, qi, g))
    kvspec = pl.BlockSpec((1, tt, LANES), lambda bi, g, qi: (bi, 0, g))
    vspec = pl.BlockSpec((1, LANES, tt), lambda bi, g, qi: (bi, g, 0)) if v_transposed else kvspec
    small = lambda shape: pl.BlockSpec(shape, lambda bi, g, qi: (0,) * len(shape))
    if mode == "diff":
        bias, tab, lq1, lk1, lq2, lk2, subln = extra
        in_specs = [qspec, kvspec, vspec,
                    pl.BlockSpec((1, 2, t, t), lambda bi, g, qi: (g, 0, 0, 0)),
                    pl.BlockSpec(memory_space=pltpu.SMEM),
                    small((1, HALF)), small((1, HALF)), small((1, HALF)), small((1, HALF)),
                    small((1, LANES))]
        args = (q, k, v, bias, tab, lq1, lk1, lq2, lk2, subln)
    else:
        fcol, frow = extra
        nh = fcol.shape[-1]
        fend = frow[:, :, t - 1::t]
        in_specs = [qspec, kvspec, vspec,
                    pl.BlockSpec((1, nh, t), lambda bi, g, qi: (bi, 0, qi)),
                    pl.BlockSpec((1, tt, nh), lambda bi, g, qi: (bi, 0, 0)),
                    pl.BlockSpec((1, nh, tt // t), lambda bi, g, qi: (bi, 0, 0))]
        args = (q, k, v, frow, fcol, fend)
    scratch = [pltpu.VMEM((2, 1, t), F32), pltpu.VMEM((2, 1, t), F32), pltpu.VMEM((2, LANES, t), F32),
               pltpu.VMEM((tt, LANES), BF16), pltpu.VMEM((LANES, tt), BF16)]
    if mode == "fox":
        scratch.append(pltpu.VMEM((1, LANES), F32))
    return pl.pallas_call(
        functools.partial(_flash2_kernel, mode=mode, t=t, lam_init=lam_init, v_transposed=v_transposed),
        grid=(b, ng, tt // t),
        in_specs=in_specs,
        out_specs=qspec,
        out_shape=jax.ShapeDtypeStruct((b, tt, w), BF16),
        scratch_shapes=scratch,
        compiler_params=_params("parallel", "parallel", "arbitrary"),
        name="flash2_" + mode,
    )(*args)


def _fox_gates_kernel(g_ref, bias_ref, lf_ref, fcol_ref, frow_ref, crow_scr, ccol_scr, *, tc, nh):
    @pl.when(pl.program_id(1) == 0)
    def _():
        crow_scr[...] = jnp.zeros(crow_scr.shape, F32)
        ccol_scr[...] = jnp.zeros(ccol_scr.shape, F32)

    lf = _log_sigmoid(g_ref[0] + bias_ref[...])
    lf_ref[0] = lf[:, :nh]
    fcol = _exact_dot(_tri(tc, True), lf) + crow_scr[...]
    frow = _exact_dot_tn(lf, _tri(tc, False)) + ccol_scr[...]
    fcol_ref[0] = fcol[:, :nh]
    frow_ref[0] = frow[:nh, :]
    crow_scr[...] = fcol[tc - 1:tc, :]
    ccol_scr[...] = frow[:, tc - 1:tc]


def _fox_gates(gates, bias_row, *, tc, nh):
    b, tt, _ = gates.shape
    return pl.pallas_call(
        functools.partial(_fox_gates_kernel, tc=tc, nh=nh),
        grid=(b, tt // tc),
        in_specs=[pl.BlockSpec((1, tc, LANES), lambda bi, c: (bi, c, 0)),
                  pl.BlockSpec((1, LANES), lambda bi, c: (0, 0))],
        out_specs=[pl.BlockSpec((1, tc, nh), lambda bi, c: (bi, c, 0)),
                   pl.BlockSpec((1, tc, nh), lambda bi, c: (bi, c, 0)),
                   pl.BlockSpec((1, nh, tc), lambda bi, c: (bi, 0, c))],
        out_shape=[jax.ShapeDtypeStruct((b, tt, nh), F32),
                   jax.ShapeDtypeStruct((b, tt, nh), F32),
                   jax.ShapeDtypeStruct((b, nh, tt), F32)],
        scratch_shapes=[pltpu.VMEM((1, LANES), F32), pltpu.VMEM((LANES, 1), F32)],
        compiler_params=_params("parallel", "arbitrary"),
        name="fox_gates",
    )(gates, bias_row)


def _hgrn2_kernel(*refs, layer, n_layers, length, n_chunks, bb, t_valid, has_state):
    if has_state:
        q_ref, f_ref, i_ref, g_ref, lbp_ref, gn_ref, s0_ref, o_ref, s_ref, st_scr = refs
    else:
        q_ref, f_ref, i_ref, g_ref, lbp_ref, gn_ref, o_ref, s_ref, st_scr = refs
    tb = pl.program_id(2)

    @pl.when(tb == 0)
    def _():
        for bi in range(bb):
            if has_state:
                st_scr[bi] = s0_ref[bi, 0].T
            else:
                st_scr[bi] = jnp.zeros(st_scr.shape[1:], F32)

    lbp = lbp_ref[...]
    e = jnp.exp(lbp - jnp.max(lbp, axis=0, keepdims=True))
    sm = e / jnp.sum(e, axis=0, keepdims=True)
    lb = jnp.sum(sm[:layer + 1], axis=0, keepdims=True)

    L = length
    C = min(SUB, L)
    tril = _tri(L, True)
    row_c = lax.broadcasted_iota(jnp.int32, (C, LANES), 0)

    states = [st_scr[bi] for bi in range(bb)]
    for c, bi in [(c, bi) for c in range(n_chunks) for bi in range(bb)]:
        rows = slice(c * L, (c + 1) * L)
        f = lb + (1.0 - lb) * _sigmoid(f_ref[bi, rows, :])
        logf = jnp.log(f)
        kk = 1.0 - f
        if t_valid is not None:
            valid = lax.broadcasted_iota(jnp.int32, (L, LANES), 0) < t_valid
            logf = jnp.where(valid, logf, 0.0)
            kk = jnp.where(valid, kk, 0.0)
        q = q_ref[bi, rows, :] * (LANES ** -0.5)
        iv = i_ref[bi, rows, :]
        bcum = _cumsum_rows(logf)
        st = states[bi]
        o_inter = _dot_nt((q * jnp.exp(bcum)).astype(BF16), st.astype(BF16))
        for sc in range(L // C):
            lo = sc * C
            q_s, b_s = q[lo:lo + C], bcum[lo:lo + C]
            o_s = o_inter[lo:lo + C]
            if sc > 0:
                ref_row = bcum[lo - 1:lo]
                qt = q_s * jnp.exp(b_s - ref_row)
                kt = kk[:lo] * jnp.exp(ref_row - bcum[:lo])
                a = _dot_nt(qt.astype(BF16), kt.astype(BF16))
                o_s = o_s + _dot(a.astype(BF16), iv[:lo].astype(BF16))
            b2_s = b_s * LOG2E
            for s in range(C):
                dec = jnp.where(row_c >= s, jnp.exp2(b2_s - b2_s[s:s + 1]), 0.0)
                a_col = jnp.sum(q_s * kk[lo + s:lo + s + 1] * dec, axis=-1, keepdims=True)
                o_s = o_s + a_col * iv[lo + s:lo + s + 1]
            gate = g_ref[bi, c * L + lo:c * L + lo + C, :]
            o_ref[bi, c * L + lo:c * L + lo + C, :] = (
                _rms(o_s, gn_ref[...]) * (gate * _sigmoid(gate))).astype(o_ref.dtype)
        b_last = bcum[L - 1:L]
        kd = kk * jnp.exp(b_last - bcum)
        states[bi] = st * jnp.exp(b_last) + _dot_tn(iv.astype(BF16), kd.astype(BF16))
    for bi in range(bb):
        st_scr[bi] = states[bi]

    @pl.when(tb == pl.num_programs(2) - 1)
    def _():
        for bi in range(bb):
            s_ref[bi, 0] = st_scr[bi].T


def _hgrn2(q, f, i, g, lb_param, gnorm, s0, *, layer, length, n_chunks, bb=1, t_valid=None):
    b, tt, w = q.shape
    nh = w // LANES
    tb = length * n_chunks
    assert b % bb == 0 and tt % tb == 0, (b, bb, tt, tb)
    xspec = pl.BlockSpec((bb, tb, LANES), lambda bi, h, ti: (bi, ti, h))
    sspec = pl.BlockSpec((bb, 1, LANES, LANES), lambda bi, h, ti: (bi, h, 0, 0))
    nl = lb_param.shape[0]
    in_specs = [xspec, xspec, xspec, xspec,
                pl.BlockSpec((nl, LANES), lambda bi, h, ti: (0, h)),
                pl.BlockSpec((1, LANES), lambda bi, h, ti: (0, 0))]
    args = [q, f, i, g, lb_param, gnorm.reshape(1, LANES)]
    if s0 is not None:
        in_specs.append(sspec)
        args.append(s0)
    return pl.pallas_call(
        functools.partial(_hgrn2_kernel, layer=layer, n_layers=nl - 1, length=length,
                          n_chunks=n_chunks, bb=bb, t_valid=t_valid, has_state=s0 is not None),
        grid=(b // bb, nh, tt // tb),
        in_specs=in_specs,
        out_specs=[xspec, sspec],
        out_shape=[jax.ShapeDtypeStruct((b, tt, w), BF16),
                   jax.ShapeDtypeStruct((b, nh, LANES, LANES), F32)],
        scratch_shapes=[pltpu.VMEM((bb, LANES, LANES), F32)],
        compiler_params=_params("parallel", "parallel", "arbitrary"),
        name="hgrn2",
    )(*args)


def _mlstm_kernel(*refs, length, n_chunks, bb, t_valid, has_state, i_lane0, f_lane0):
    if has_state:
        (q_ref, k_ref, v_ref, og_ref, gt_ref, gb_ref, gn_ref, c0_ref, n0_ref, m0_ref,
         o_ref, c_ref, n_ref, m_ref, c_scr, n_scr, m_scr) = refs
    else:
        (q_ref, k_ref, v_ref, og_ref, gt_ref, gb_ref, gn_ref,
         o_ref, c_ref, n_ref, m_ref, c_scr, n_scr, m_scr) = refs
    hp, tb = pl.program_id(1), pl.program_id(2)

    @pl.when(tb == 0)
    def _():
        top = lax.broadcasted_iota(jnp.int32, (LANES, LANES), 0) < HALF
        for bi in range(bb):
            if has_state:
                c0 = c0_ref[bi, 0]
                c_scr[bi] = jnp.concatenate([jnp.where(top, c0, 0.0), jnp.where(top, 0.0, c0)], axis=1)
                n_scr[bi] = n0_ref[bi, 0]
                m_scr[bi] = m0_ref[bi, 0]
            else:
                c_scr[bi] = jnp.zeros(c_scr.shape[1:], F32)
                n_scr[bi] = jnp.zeros(n_scr.shape[1:], F32)
                m_scr[bi] = jnp.zeros(m_scr.shape[1:], F32)

    L = length
    lane = lax.broadcasted_iota(jnp.int32, (L, LANES), 1)
    lane2 = lax.broadcasted_iota(jnp.int32, (L, 2 * L), 1)
    row2 = lax.broadcasted_iota(jnp.int32, (L, 2 * L), 0)
    left2 = lane2 < L
    causal2 = jnp.where(left2, lane2, lane2 - L) <= row2
    left_v = lax.broadcasted_iota(jnp.int32, (L, 2 * LANES), 1) < LANES
    left_q = lane < HALF
    nlane = lax.broadcasted_iota(jnp.int32, (1, LANES), 1)
    crow = lax.broadcasted_iota(jnp.int32, (LANES, 2 * LANES), 0)
    ccol = lax.broadcasted_iota(jnp.int32, (LANES, 2 * LANES), 1)
    own_block = (crow < HALF) == (ccol < LANES)
    is_f = (lane >= f_lane0) & (lane < f_lane0 + 4)
    two = lambda cond, a: jnp.where(cond, a[0], a[1])
    cs = [c_scr[bi] for bi in range(bb)]
    ns = [n_scr[bi] for bi in range(bb)]
    ms = [m_scr[bi] for bi in range(bb)]
    for c, bi in [(c, bi) for c in range(n_chunks) for bi in range(bb)]:
        rows = slice(c * L, (c + 1) * L)
        gpre = gt_ref[bi, rows, :] + gb_ref[...]
        x = jnp.where(is_f, _log_sigmoid(gpre), gpre)
        if t_valid is not None:
            valid = lax.broadcasted_iota(jnp.int32, (L, LANES), 0) < t_valid
            x = jnp.where(valid, x, jnp.where(is_f, 0.0, NEG))
        bcol_all = _cumsum_rows(x)
        xt = x.T
        brow_all = bcol_all.T
        srow = lax.broadcasted_iota(jnp.int32, xt.shape, 0)
        pick = lambda a, l: jnp.sum(jnp.where(lane == l, a, 0.0), axis=-1, keepdims=True)
        pick_t = lambda a, l: jnp.sum(jnp.where(srow == l, a, 0.0), axis=0, keepdims=True)
        heads = (2 * hp, 2 * hp + 1)
        b_col = [pick(bcol_all, f_lane0 + h) for h in heads]
        ig_col = [pick(x, i_lane0 + h) for h in heads]
        b_row2 = jnp.concatenate([pick_t(brow_all, f_lane0 + h) for h in heads], axis=1)
        ig_row2 = jnp.concatenate([pick_t(xt, i_lane0 + h) for h in heads], axis=1)
        m_old = [jnp.sum(jnp.where(nlane == j, ms[bi], 0.0), axis=-1, keepdims=True) for j in range(2)]
        d2 = jnp.where(causal2, two(left2, b_col) - b_row2 + ig_row2, NEG)
        inter = [b_col[j] + m_old[j] for j in range(2)]
        m_t = [jnp.maximum(inter[0], jnp.max(jnp.where(left2, d2, NEG), axis=-1, keepdims=True)),
               jnp.maximum(inter[1], jnp.max(jnp.where(left2, NEG, d2), axis=-1, keepdims=True))]
        w_inter = [jnp.exp(inter[j] - m_t[j]) for j in range(2)]
        wmat2 = jnp.exp(d2 - two(left2, m_t))

        q = q_ref[bi, rows, :]
        k = k_ref[bi, rows, :] * (HALF ** -0.5)
        v = v_ref[bi, rows, :]
        qb = q.astype(BF16)
        cst, nst = cs[bi], ns[bi]
        k_bd = jnp.concatenate([jnp.where(left_q, k, 0.0), jnp.where(left_q, 0.0, k)], axis=0)
        v_bd = jnp.concatenate([jnp.where(left_v, v, 0.0), jnp.where(left_v, 0.0, v)], axis=0)
        qk2 = _dot_nt(qb, k_bd.astype(BF16)) * wmat2
        num2 = (two(left_v, w_inter) * _dot(qb, cst.astype(BF16))
                + _dot(qk2.astype(BF16), v_bd.astype(BF16)))
        qn = q * nst
        den = [w_inter[0] * jnp.sum(jnp.where(left_q, qn, 0.0), axis=-1, keepdims=True)
               + jnp.sum(jnp.where(left2, qk2, 0.0), axis=-1, keepdims=True),
               w_inter[1] * jnp.sum(jnp.where(left_q, 0.0, qn), axis=-1, keepdims=True)
               + jnp.sum(jnp.where(left2, 0.0, qk2), axis=-1, keepdims=True)]
        floor = [jnp.maximum(jnp.abs(den[j]), jnp.exp(-m_t[j])) for j in range(2)]
        hd2 = num2 / two(left_v, floor)
        og = og_ref[bi, rows, :]
        for j in range(2):
            cols = slice(j * LANES, (j + 1) * LANES)
            o_ref[bi, rows, cols] = (_rms(hd2[:, cols], gn_ref[...]) * _sigmoid(og[:, cols])).astype(o_ref.dtype)

        m_new = [m_t[j][L - 1:L] for j in range(2)]
        b_last = [b_col[j][L - 1:L] for j in range(2)]
        w_c = [jnp.exp(b_last[j] + m_old[j] - m_new[j]) for j in range(2)]
        w_s = [jnp.exp(b_last[j] - b_col[j] + ig_col[j] - m_new[j]) for j in range(2)]
        ks = k * two(left_q, w_s)
        upd = _dot_tn(ks.astype(BF16), v.astype(BF16))
        cs[bi] = two(crow < HALF, w_c) * cst + jnp.where(own_block, upd, 0.0)
        ns[bi] = two(nlane < HALF, w_c) * nst + jnp.sum(ks, axis=0, keepdims=True)
        ms[bi] = jnp.where(nlane == 0, m_new[0], jnp.where(nlane == 1, m_new[1], ms[bi]))
    for bi in range(bb):
        c_scr[bi], n_scr[bi], m_scr[bi] = cs[bi], ns[bi], ms[bi]

    @pl.when(tb == pl.num_programs(2) - 1)
    def _():
        for bi in range(bb):
            c_ref[bi, 0] = c_scr[bi, :, :LANES] + c_scr[bi, :, LANES:]
            n_ref[bi, 0] = n_scr[bi]
            m_ref[bi, 0] = m_scr[bi]


def _mlstm(qk, v, og, gates, gate_bias, gnorm, state, *, length, n_chunks, i_lane0, f_lane0, bb=1,
           t_valid=None):
    b, tt, _ = qk.shape
    tb = length * n_chunks
    assert b % bb == 0 and tt % tb == 0, (b, bb, tt, tb)
    npair = 2
    qspec = pl.BlockSpec((bb, tb, LANES), lambda bi, hp, ti: (bi, ti, hp))
    kspec = pl.BlockSpec((bb, tb, LANES), lambda bi, hp, ti: (bi, ti, npair + hp))
    vspec = pl.BlockSpec((bb, tb, 2 * LANES), lambda bi, hp, ti: (bi, ti, hp))
    gspec = pl.BlockSpec((bb, tb, LANES), lambda bi, hp, ti: (bi, ti, 0))
    row = pl.BlockSpec((1, LANES), lambda bi, hp, ti: (0, 0))
    cspec = pl.BlockSpec((bb, 1, LANES, LANES), lambda bi, hp, ti: (bi, hp, 0, 0))
    nspec = pl.BlockSpec((bb, 1, 1, LANES), lambda bi, hp, ti: (bi, hp, 0, 0))
    in_specs = [qspec, kspec, vspec, vspec, gspec, row, row]
    args = [qk, qk, v, og, gates, gate_bias, gnorm.reshape(1, LANES)]
    if state is not None:
        in_specs += [cspec, nspec, nspec]
        args += list(state)
    return pl.pallas_call(
        functools.partial(_mlstm_kernel, length=length, n_chunks=n_chunks, bb=bb, t_valid=t_valid,
                          has_state=state is not None, i_lane0=i_lane0, f_lane0=f_lane0),
        grid=(b // bb, npair, tt // tb),
        in_specs=in_specs,
        out_specs=[vspec, cspec, nspec, nspec],
        out_shape=[jax.ShapeDtypeStruct((b, tt, 4 * LANES), BF16),
                   jax.ShapeDtypeStruct((b, npair, LANES, LANES), F32),
                   jax.ShapeDtypeStruct((b, npair, 1, LANES), F32),
                   jax.ShapeDtypeStruct((b, npair, 1, LANES), F32)],
        scratch_shapes=[pltpu.VMEM((bb, LANES, 2 * LANES), F32), pltpu.VMEM((bb, 1, LANES), F32),
                        pltpu.VMEM((bb, 1, LANES), F32)],
        compiler_params=_params("parallel", "parallel", "arbitrary"),
        name="mlstm",
    )(*args)


ROWS = 16


def _pad_rows(x, rows):
    return jnp.concatenate([x, jnp.zeros((rows - x.shape[0], x.shape[1]), x.dtype)], axis=0)


def _decode_diff_kernel(pt_ref, q_ref, kn_ref, vn_ref, tab_ref, lq1, lk1, lq2, lk2, subln_ref, *rest,
                        n_pages, page, n_heads, lam_init):
    del pt_ref
    k_refs, v_refs = rest[:n_pages], rest[n_pages:2 * n_pages]
    o_ref, bias_scr = rest[2 * n_pages], rest[2 * n_pages + 1]
    past = n_pages * page
    pw = page * n_heads
    rr = lax.broadcasted_iota(jnp.int32, (ROWS, pw), 0) % 8
    ll = lax.broadcasted_iota(jnp.int32, (ROWS, pw), 1)
    own = (ll % n_heads == rr) & (rr < n_heads)
    rcol = lax.broadcasted_iota(jnp.int32, (ROWS, 1), 0) % 8

    @pl.when(pl.program_id(0) == 0)
    def _():
        for p in range(n_pages):
            n = past - (p * page + ll // n_heads)
            acc = jnp.zeros((ROWS, pw), F32)
            for h in range(n_heads):
                acc = jnp.where(rr == h, _t5_bias_value(n, tab_ref, h), acc)
            bias_scr[:, p * pw:(p + 1) * pw] = acc

    lane = lax.broadcasted_iota(jnp.int32, (8, LANES), 1)
    q8 = _pad_rows(q_ref[0], 8) * (HALF ** -0.5)
    qm = jnp.concatenate([jnp.where(lane < HALF, q8, 0.0), jnp.where(lane >= HALF, q8, 0.0)], axis=0)
    qb = qm.astype(BF16)
    kn8 = _pad_rows(kn_ref[0], 8)
    bias_new = jnp.zeros((ROWS, 1), F32)
    for h in range(n_heads):
        bias_new = jnp.where(rcol == h, tab_ref[0, h], bias_new)
    s_new = jnp.sum(qm * jnp.concatenate([kn8, kn8], axis=0), axis=-1, keepdims=True) + bias_new
    s_pages = [jnp.where(own, _dot_nt(qb, k_refs[p][...].astype(BF16)) + bias_scr[:, p * pw:(p + 1) * pw], NEG)
               for p in range(n_pages)]
    m = s_new
    for s in s_pages:
        m = jnp.maximum(m, jnp.max(s, axis=-1, keepdims=True))
    e_new = jnp.exp(s_new - m)
    e_pages = [jnp.exp(s - m) for s in s_pages]
    l = e_new
    for e in e_pages:
        l = l + jnp.sum(e, axis=-1, keepdims=True)
    lam = _lam_value(lq1, lk1, lq2, lk2, lam_init)
    inv = 1.0 / l
    comb = lambda a: a[0:8] - lam * a[8:16]
    out = comb(e_new * inv) * _pad_rows(vn_ref[0], 8)
    for p in range(n_pages):
        wgt = _pad_rows(comb(e_pages[p] * inv), ROWS).astype(BF16)
        out = out + _dot(wgt, v_refs[p][...].astype(BF16))[0:8]
    y = _rms(out, subln_ref[...]) * (1.0 - lam_init)
    o_ref[0] = y[0:n_heads].astype(o_ref.dtype)


def _decode_diff(page_table, q, k_new, v_new, k_pool, v_pool, rel_bias, lq1, lk1, lq2, lk2, subln, *, lam_init):
    ns, n_pages = page_table.shape
    n_heads = q.shape[1]
    pw = k_pool.shape[1]
    row = pl.BlockSpec((1, n_heads, LANES), lambda b, pt: (b, 0, 0))
    small = lambda shape: pl.BlockSpec(shape, lambda b, pt: (0,) * len(shape))
    page_spec = lambda p: pl.BlockSpec((None, pw, LANES), lambda b, pt: (pt[b, p], 0, 0))
    in_specs = [row, row, row, pl.BlockSpec(memory_space=pltpu.SMEM),
                small((1, HALF)), small((1, HALF)), small((1, HALF)), small((1, HALF)), small((1, LANES))]
    in_specs += [page_spec(p) for p in range(n_pages)] * 2
    grid_spec = pltpu.PrefetchScalarGridSpec(
        num_scalar_prefetch=1, grid=(ns,), in_specs=in_specs, out_specs=row,
        scratch_shapes=[pltpu.VMEM((ROWS, n_pages * pw), F32)])
    return pl.pallas_call(
        functools.partial(_decode_diff_kernel, n_pages=n_pages, page=pw // n_heads, n_heads=n_heads,
                          lam_init=lam_init),
        grid_spec=grid_spec,
        out_shape=jax.ShapeDtypeStruct((ns, n_heads, LANES), BF16),
        compiler_params=_params("arbitrary"),
        name="decode_diff",
    )(page_table, q, k_new, v_new, rel_bias, lq1, lk1, lq2, lk2, subln,
      *([k_pool] * n_pages), *([v_pool] * n_pages))


def _decode_fox_kernel(pt_ref, q_ref, kn_ref, vn_ref, g_ref, gb_ref, *rest, n_pages, page, n_heads):
    del pt_ref
    f_refs, k_refs, v_refs = rest[:n_pages], rest[n_pages:2 * n_pages], rest[2 * n_pages:3 * n_pages]
    o_ref, lf_ref = rest[3 * n_pages], rest[3 * n_pages + 1]
    w = n_heads * HALF
    r_i = lax.broadcasted_iota(jnp.int32, (ROWS, w), 0)
    lane = lax.broadcasted_iota(jnp.int32, (ROWS, w), 1)
    q = q_ref[0] * (HALF ** -0.5)
    qbig = jnp.where(lane // HALF == r_i, q, 0.0)
    qb = qbig.astype(BF16)
    lf_row = _log_sigmoid(g_ref[0] + gb_ref[...])
    lf_ref[0] = lf_row[:, :n_heads]
    gr = lax.broadcasted_iota(jnp.int32, (ROWS, LANES), 0)
    gl = lax.broadcasted_iota(jnp.int32, (ROWS, LANES), 1)
    lf_col = jnp.sum(jnp.where((gl == gr) & (gr < n_heads), lf_row, 0.0), axis=-1, keepdims=True)
    triu = _tri(page, False)
    carry = jnp.zeros((ROWS, 1), F32)
    prefix = []
    for p in range(n_pages):
        pp = _exact_dot_r(_pad_rows(f_refs[p][...], ROWS), triu) + carry
        prefix.append(pp)
        carry = pp[:, page - 1:page]
    s_new = jnp.sum(qbig * kn_ref[0], axis=-1, keepdims=True)
    s_pages = [_dot(qb, k_refs[p][...].astype(BF16)) + (lf_col + carry - prefix[p])
               for p in range(n_pages)]
    m = s_new
    for s in s_pages:
        m = jnp.maximum(m, jnp.max(s, axis=-1, keepdims=True))
    e_new = jnp.exp(s_new - m)
    e_pages = [jnp.exp(s - m) for s in s_pages]
    l = e_new
    for e in e_pages:
        l = l + jnp.sum(e, axis=-1, keepdims=True)
    inv = 1.0 / l
    out = (e_new * inv) * vn_ref[0]
    for p in range(n_pages):
        out = out + _dot_nt((e_pages[p] * inv).astype(BF16), v_refs[p][...].astype(BF16))
    o_ref[0] = jnp.sum(jnp.where(lane // HALF == r_i, out, 0.0), axis=0, keepdims=True).astype(o_ref.dtype)


def _decode_fox(page_table, q, k_new, v_new, gates, gate_bias, lf_pool_t, k_pool_t, v_pool_t):
    ns, n_pages = page_table.shape
    _, w, page = k_pool_t.shape
    n_heads = w // HALF
    row = pl.BlockSpec((1, 1, w), lambda b, pt: (b, 0, 0))
    grow = pl.BlockSpec((1, 1, LANES), lambda b, pt: (b, 0, 0))
    page_spec = lambda p: pl.BlockSpec((None, w, page), lambda b, pt: (pt[b, p], 0, 0))
    f_spec = lambda p: pl.BlockSpec((None, n_heads, page), lambda b, pt: (pt[b, p], 0, 0))
    in_specs = [row, row, row, grow, pl.BlockSpec((1, LANES), lambda b, pt: (0, 0))]
    in_specs += [f_spec(p) for p in range(n_pages)] + [page_spec(p) for p in range(n_pages)] * 2
    grid_spec = pltpu.PrefetchScalarGridSpec(
        num_scalar_prefetch=1, grid=(ns,), in_specs=in_specs,
        out_specs=[row, pl.BlockSpec((1, 1, n_heads), lambda b, pt: (b, 0, 0))])
    return pl.pallas_call(
        functools.partial(_decode_fox_kernel, n_pages=n_pages, page=page, n_heads=n_heads),
        grid_spec=grid_spec,
        out_shape=[jax.ShapeDtypeStruct((ns, 1, w), BF16), jax.ShapeDtypeStruct((ns, 1, n_heads), F32)],
        compiler_params=_params("arbitrary"),
        name="decode_fox",
    )(page_table, q, k_new, v_new, gates, gate_bias,
      *([lf_pool_t] * n_pages), *([k_pool_t] * n_pages), *([v_pool_t] * n_pages))


def kernel(x_prompt, x_sample, cache_k_a, cache_v_a, state_s_b, cache_k_c, cache_v_c, cache_logf_c,
           state_c_d, state_n_d, state_m_d, page_table, norm_mix, w_in_even, lambda_q1, lambda_k1,
           lambda_q2, lambda_k2, subln_a, rel_bias, lb_param, gnorm_b, w_out_even, w_in_odd, b_f_c,
           b_i_d, b_f_d, gnorm_d, w_out_odd, norm_mlp, w_up, w_down, norm_final):
    B, T, D = x_prompt.shape
    S = x_sample.shape[0]
    n_pool, page = cache_k_a.shape[1], cache_k_a.shape[2]
    h_a, h_c, h_d = cache_k_a.shape[3], cache_k_c.shape[3], state_c_d.shape[2]
    w_a = h_a * LANES
    lam_init = 0.8 - 0.6 * math.exp(-0.3 * 0)
    tq = 512

    w_in0 = w_in_even[0].astype(BF16)
    wo = w_in_odd[0]
    c = [0]
    for sz in (h_c * HALF, h_c * HALF, h_c * HALF, h_c, h_d * HALF, h_d * HALF, h_d * LANES, h_d, h_d,
               h_d * LANES):
        c.append(c[-1] + sz)
    col = lambda i: wo[:, c[i]:c[i + 1]]
    gate_cols = jnp.concatenate([col(3), col(7), col(8)], axis=1)
    gate_cols = jnp.pad(gate_cols, ((0, 0), (0, w_a - gate_cols.shape[1])))
    w_in1 = jnp.concatenate([col(0), col(1), col(2), col(4), col(5), col(6), col(9), gate_cols],
                            axis=1).astype(BF16)
    gate_bias = jnp.pad(jnp.concatenate([b_f_c[0], b_i_d[0], b_f_d[0]]), (0, LANES - h_c - 2 * h_d))
    gate_bias = gate_bias.reshape(1, LANES)
    i_lane0, f_lane0 = h_c, h_c + h_d
    w_out0, w_out1 = w_out_even[0].astype(BF16), w_out_odd[0].astype(BF16)
    w_up_b, w_down_b = w_up.astype(BF16), w_down.astype(BF16)
    lq1, lk1, lq2, lk2 = (a.reshape(1, HALF) for a in (lambda_q1[0], lambda_k1[0], lambda_q2[0], lambda_k2[0]))
    subln = subln_a[0].reshape(1, LANES)

    M = B * T
    xp = x_prompt.reshape(M, D)
    qa, ka, va, qb, fb, ib, gb, ka_hm, va_hm = _norm_matmul(
        xp, norm_mix[0], w_in0, tm=512, tn=w_a, dtypes=[F32, BF16, F32, F32, F32, F32, F32], head_major=(1, 2))
    r3 = lambda a: a.reshape(B, T, w_a)
    bias = _bias_tiles(rel_bias, t=tq)
    o_a = _flash2("diff", r3(qa), r3(ka), r3(va), (bias, rel_bias, lq1, lk1, lq2, lk2, subln),
                  t=tq, lam_init=lam_init)
    o_b, p_s_b = _hgrn2(r3(qb), r3(fb), r3(ib), r3(gb), lb_param, gnorm_b[0], None,
                        layer=0, length=CHUNK, n_chunks=8, bb=2)
    x2 = _mix_mlp(o_a.reshape(M, w_a), o_b.reshape(M, w_a), w_out0, xp,
                  norm_mlp[0], w_up_b[0], w_down_b[0], tm=512)
    w_kv_t = jnp.concatenate([col(1), col(2)], axis=1).T.astype(BF16)
    qc, kc, qkd, vd, od, gts, kc_t, vc_t = _norm_matmul(
        x2, norm_mix[1], w_in1, tm=512, tn=w_a, dtypes=[F32, BF16, None, F32, F32, F32, F32], wt=w_kv_t, seq=T)
    p_lf_c, fcol, frow = _fox_gates(r3(gts), gate_bias, tc=512, nh=h_c)
    o_c = _flash2("fox", r3(qc), r3(kc), vc_t, (fcol, frow), t=tq, v_transposed=True)
    o_d, p_c, p_n, p_m = _mlstm(r3(qkd), r3(vd), r3(od), r3(gts), gate_bias, gnorm_d[0], None,
                                length=CHUNK, n_chunks=4, bb=2, i_lane0=i_lane0, f_lane0=f_lane0)
    y_prompt = _mix_mlp(o_c.reshape(M, w_a), o_d.reshape(M, w_a), w_out1, x2,
                        norm_mlp[1], w_up_b[1], w_down_b[1], tm=512, final_g=norm_final).reshape(B, T, D)

    PADT = 8
    DEC_BB = 16
    xs = x_sample.reshape(S, D)
    sqa, ska, sva, sqb, sfb, sib, sgb = _norm_matmul(xs, norm_mix[0], w_in0, tm=S, tn=w_a)
    s3 = lambda a: a.reshape(S, 1, w_a)
    padt = lambda a: jnp.pad(a.reshape(S, 1, w_a), ((0, 0), (0, PADT - 1), (0, 0)))
    sh = lambda a: a.reshape(S, h_a, LANES)
    so_a = _decode_diff(page_table, sh(sqa), sh(ska), sh(sva),
                        cache_k_a.reshape(n_pool, page * h_a, LANES), cache_v_a.reshape(n_pool, page * h_a, LANES),
                        rel_bias, lq1, lk1, lq2, lk2, subln, lam_init=lam_init)
    so_b, s_s_b = _hgrn2(padt(sqb), padt(sfb), padt(sib), padt(sgb), lb_param, gnorm_b[0], state_s_b[0],
                         layer=0, length=PADT, n_chunks=1, bb=DEC_BB, t_valid=1)
    sx2 = _mix_mlp(so_a.reshape(S, w_a), so_b[:, 0], w_out0, xs, norm_mlp[0], w_up_b[0], w_down_b[0], tm=S)
    sqc, skc, svc, sqkd, svd, sod, sgts = _norm_matmul(sx2, norm_mix[1], w_in1, tm=S, tn=w_a)
    lf_pool_t = jnp.swapaxes(cache_logf_c[0], 1, 2)
    pool_t = lambda a: jnp.transpose(a[0], (0, 2, 3, 1)).reshape(n_pool, w_a, page)
    so_c, s_lf_c = _decode_fox(page_table, s3(sqc), s3(skc), s3(svc), sgts[:, :LANES].reshape(S, 1, LANES),
                               gate_bias, lf_pool_t, pool_t(cache_k_c), pool_t(cache_v_c))
    npair = h_d // 2
    m0 = jnp.pad(state_m_d[0].reshape(S, npair, 1, 2), ((0, 0), (0, 0), (0, 0), (0, LANES - 2)))
    so_d, s_c, s_n, s_m = _mlstm(padt(sqkd), padt(svd), padt(sod), padt(sgts), gate_bias, gnorm_d[0],
                                 (state_c_d[0].reshape(S, npair, LANES, LANES),
                                  state_n_d[0].reshape(S, npair, 1, LANES), m0),
                                 length=PADT, n_chunks=1, i_lane0=i_lane0, f_lane0=f_lane0, bb=DEC_BB,
                                 t_valid=1)
    y_sample = _mix_mlp(so_c.reshape(S, w_a), so_d[:, 0], w_out1, sx2, norm_mlp[1], w_up_b[1], w_down_b[1],
                        tm=S, final_g=norm_final).reshape(S, 1, D)

    dk_d = state_c_d.shape[3]
    heads = lambda a, nb, tt, nh: a.reshape(1, nb, tt, nh, w_a // nh)
    unpair_c = lambda a, nb: a.reshape(1, nb, h_d, dk_d, LANES)
    unpair_n = lambda a, nb: a.reshape(1, nb, h_d, dk_d)
    unpair_m = lambda a, nb: a[..., :2].reshape(1, nb, h_d)
    from_t = lambda a: jnp.transpose(a.reshape(1, B, h_c, HALF, T), (0, 1, 4, 2, 3))
    return (y_prompt, y_sample,
            ka_hm.reshape(1, B, T, h_a, LANES), va_hm.reshape(1, B, T, h_a, LANES), p_s_b[None],
            from_t(kc_t), from_t(vc_t), p_lf_c[None],
            unpair_c(p_c, B), unpair_n(p_n, B), unpair_m(p_m, B),
            heads(ska, S, 1, h_a), heads(sva, S, 1, h_a), s_s_b[None],
            heads(skc, S, 1, h_c), heads(svc, S, 1, h_c), s_lf_c[None],
            unpair_c(s_c, S), unpair_n(s_n, S), unpair_m(s_m, S))
```
